```python
import math
import jax, jax.numpy as jnp
from jax import lax
import numpy as np

D_MODEL = 2048
BATCH = 2
SEQ = 8192
DEPTH = 2

N_MEM = 256
RET_HEADS = 8
RET_QK_DIM = D_MODEL // (2 * RET_HEADS)
RET_V_DIM = D_MODEL // RET_HEADS
RET_Q = RET_HEADS * RET_QK_DIM
RET_V = RET_HEADS * RET_V_DIM
RET_CHUNK = 128
ROPE_BASE = 10000.0
SSD_D_INNER = 2 * D_MODEL
SSD_HEAD_DIM = 64
SSD_HEADS = SSD_D_INNER // SSD_HEAD_DIM
SSD_GROUPS = 8
SSD_HEADS_PER_GROUP = SSD_HEADS // SSD_GROUPS
SSD_STATE = 128
SSD_BC = SSD_GROUPS * SSD_STATE
SSD_CONV = 5
SSD_CHUNK = 128
IN_WIDTHS = (RET_Q, RET_Q, RET_V, RET_V,
             SSD_D_INNER, SSD_D_INNER, SSD_BC, SSD_BC,
             2 * SSD_HEADS,
             D_MODEL, D_MODEL)
IN_WIDTH = sum(IN_WIDTHS)
XATTN_HEADS = 4
XATTN_HEAD_DIM = D_MODEL // XATTN_HEADS
N_EXPERTS = 32
TOP_K = 4
EXPERT_DIM = D_MODEL // 2
SWIGLU_LIMIT = 7.0
SWIGLU_ALPHA = 1.702
MOE_BLOCK = 256
DEEPNORM_ALPHA = (2.0 * DEPTH) ** 0.25
DEEPNORM_BETA = (8.0 * DEPTH) ** -0.25
LN_EPS = 1e-5

kernel_name = 'bidir_hybrid_retention_ssd_moe_encoder'


def _layernorm(x, g, b):
    xf = x.astype(jnp.float32)
    mu = jnp.mean(xf, -1, keepdims=True)
    var = jnp.mean(jnp.square(xf - mu), -1, keepdims=True)
    return ((xf - mu) * lax.rsqrt(var + LN_EPS) * g + b).astype(x.dtype)


def _head_norm(t):
    mu = jnp.mean(t, -1, keepdims=True)
    var = jnp.mean(jnp.square(t - mu), -1, keepdims=True)
    return (t - mu) * lax.rsqrt(var + LN_EPS)


def _rotary(t, pos):
    half = t.shape[-1] // 2
    inv_freq = ROPE_BASE ** (-jnp.arange(half, dtype=jnp.float32) / half)
    ang = pos.astype(jnp.float32)[:, :, None] * inv_freq
    cos = jnp.cos(ang)[:, :, None, :]
    sin = jnp.sin(ang)[:, :, None, :]
    t1, t2 = t[..., :half], t[..., half:]
    return jnp.concatenate([t1 * cos - t2 * sin, t1 * sin + t2 * cos], -1)


def _retention_dir(q, k, v, log_gamma, include_diag):
    b, h, s, dk = q.shape
    dv = v.shape[-1]
    c = RET_CHUNK
    n = s // c

    def chunks(t):
        return jnp.moveaxis(t.reshape(b, h, n, c, t.shape[-1]), 2, 0)

    idx = jnp.arange(c, dtype=jnp.float32)
    dist = idx[:, None] - idx[None, :]
    mask = (dist >= 0) if include_diag else (dist > 0)
    intra_decay = jnp.where(mask, jnp.exp(log_gamma[:, None, None] * jnp.maximum(dist, 0.0)), 0.0)
    q_decay = jnp.exp(log_gamma[:, None] * (idx + 1.0))[..., None]
    k_decay = jnp.exp(log_gamma[:, None] * (c - 1.0 - idx))[..., None]
    chunk_decay = jnp.exp(log_gamma * c)[:, None, None]

    def step(state, inp):
        qn, kn, vn = inp
        scores = jnp.einsum('bhid,bhjd->bhij', qn, kn) * intra_decay
        y = (jnp.einsum('bhij,bhje->bhie', scores, vn)
             + jnp.einsum('bhid,bhde->bhie', qn * q_decay, state))
        state = chunk_decay * state + jnp.einsum('bhjd,bhje->bhde', kn * k_decay, vn)
        return state, y

    state0 = jnp.zeros((b, h, dk, dv), jnp.float32)
    _, y = lax.scan(step, state0, (chunks(q), chunks(k), chunks(v)))
    return jnp.moveaxis(y, 0, 2).reshape(b, h, s, dv)


def _retention(q, k, v, log_gamma):
    fwd = _retention_dir(q, k, v, log_gamma, True)
    bwd = _retention_dir(jnp.flip(q, 2), jnp.flip(k, 2), jnp.flip(v, 2), log_gamma, False)
    return fwd + jnp.flip(bwd, 2)


def _ssd_dir(x, dt, a, bm, cm, include_diag):
    b, s, g, r, p = x.shape
    nst = bm.shape[-1]
    l = SSD_CHUNK
    n = s // l

    def chunks(t):
        return jnp.moveaxis(t.reshape((b, n, l) + t.shape[2:]), 1, 0)

    idx = jnp.arange(l)
    mask = (idx[:, None] >= idx[None, :]) if include_diag else (idx[:, None] > idx[None, :])
    mask = mask[None, :, :, None, None]

    def step(hst, inp):
        xn, dtn, bn, cn = inp
        acum = jnp.cumsum(dtn * a, axis=1)
        seg = acum[:, :, None] - acum[:, None, :]
        lmat = jnp.exp(jnp.where(mask, seg, -jnp.inf))
        cb = jnp.einsum('bign,bjgn->bijg', cn, bn)
        w = cb[..., None] * lmat * dtn[:, None]
        y = jnp.einsum('bijgr,bjgrp->bigrp', w, xn)
        y = y + jnp.einsum('bign,bgrpn->bigrp', cn, hst) * jnp.exp(acum)[..., None]
        decay_end = jnp.exp(acum[:, -1:] - acum) * dtn
        hst = (jnp.exp(acum[:, -1])[..., None, None] * hst
               + jnp.einsum('bjgn,bjgrp->bgrpn', bn, decay_end[..., None] * xn))
        return hst, y

    h0 = jnp.zeros((b, g, r, p, nst), jnp.float32)
    _, y = lax.scan(step, h0, (chunks(x), chunks(dt), chunks(bm), chunks(cm)))
    return jnp.moveaxis(y, 0, 1).reshape(b, s, g, r, p)


def _centred_dwconv(x, w, bias):
    pad = w.shape[0] // 2
    y = lax.conv_general_dilated(x, w[:, None, :].astype(x.dtype), window_strides=(1,),
                                 padding=[(pad, pad)], dimension_numbers=('NWC', 'WIO', 'NWC'),
                                 feature_group_count=x.shape[-1])
    return y + bias


def _hybrid_mixer(h, pos, w_in, conv_w, conv_b, dt_bias, a_log, d_skip, ssd_norm_g,
                  w_ret_o, w_ssd_o, w_mix_o):
    b, s, _ = h.shape
    f32 = jnp.float32
    points = []
    acc = 0
    for wdt in IN_WIDTHS[:-1]:
        acc += wdt
        points.append(acc)
    (q, k, v, g_ret, z, xs, bm, cm, dt_raw, gate_ret, gate_ssd) = jnp.split(h @ w_in, points, axis=-1)

    q = _rotary(q.astype(f32).reshape(b, s, RET_HEADS, RET_QK_DIM), pos)
    k = _rotary(k.astype(f32).reshape(b, s, RET_HEADS, RET_QK_DIM), pos) * (RET_QK_DIM ** -0.5)
    v = v.astype(f32).reshape(b, s, RET_HEADS, RET_V_DIM)
    log_gamma = jnp.log1p(-jnp.exp2(-5.0 - jnp.arange(RET_HEADS, dtype=f32)))
    ret = _retention(q.transpose(0, 2, 1, 3), k.transpose(0, 2, 1, 3), v.transpose(0, 2, 1, 3), log_gamma)
    ret = _head_norm(ret.transpose(0, 2, 1, 3)).reshape(b, s, RET_V)
    y_ret = (ret * jax.nn.silu(g_ret.astype(f32))).astype(h.dtype) @ w_ret_o

    xbc = jax.nn.silu(_centred_dwconv(jnp.concatenate([xs, bm, cm], -1), conv_w, conv_b))
    xs, bm, cm = jnp.split(xbc, [SSD_D_INNER, SSD_D_INNER + SSD_BC], axis=-1)
    xh = xs.astype(f32).reshape(b, s, SSD_GROUPS, SSD_HEADS_PER_GROUP, SSD_HEAD_DIM)
    bm = bm.astype(f32).reshape(b, s, SSD_GROUPS, SSD_STATE)
    cm = cm.astype(f32).reshape(b, s, SSD_GROUPS, SSD_STATE)
    dt = jax.nn.softplus(dt_raw.astype(f32).reshape(b, s, 2, SSD_HEADS) + dt_bias.astype(f32))
    dt = dt.reshape(b, s, 2, SSD_GROUPS, SSD_HEADS_PER_GROUP)
    a = -jnp.exp(a_log.astype(f32)).reshape(2, SSD_GROUPS, SSD_HEADS_PER_GROUP)
    flip = lambda t: jnp.flip(t, axis=1)
    y = (_ssd_dir(xh, dt[:, :, 0], a[0], bm, cm, True)
         + flip(_ssd_dir(flip(xh), flip(dt[:, :, 1]), a[1], flip(bm), flip(cm), False))
         + d_skip.astype(f32).reshape(SSD_GROUPS, SSD_HEADS_PER_GROUP, 1) * xh)
    y = y.reshape(b, s, SSD_D_INNER) * jax.nn.silu(z.astype(f32))
    yg = y.reshape(b, s, SSD_GROUPS, SSD_D_INNER // SSD_GROUPS)
    yg = yg * lax.rsqrt(jnp.mean(jnp.square(yg), -1, keepdims=True) + LN_EPS)
    y_ssd = (yg.reshape(b, s, SSD_D_INNER) * ssd_norm_g).astype(h.dtype) @ w_ssd_o

    merged = jax.nn.sigmoid(gate_ret) * y_ret + jax.nn.sigmoid(gate_ssd) * y_ssd
    return merged @ w_mix_o


def _memory_xattn(h, mem_n, w_xq, w_xkv, w_xo):
    b, s, d = h.shape
    m = mem_n.shape[1]
    q = (h @ w_xq).reshape(b, s, XATTN_HEADS, XATTN_HEAD_DIM)
    kmem, vmem = jnp.split(mem_n @ w_xkv, 2, axis=-1)
    kmem = kmem.reshape(b, m, XATTN_HEADS, XATTN_HEAD_DIM)
    vmem = vmem.reshape(b, m, XATTN_HEADS, XATTN_HEAD_DIM)
    scores = jnp.einsum('bshd,bmhd->bhsm', q, kmem).astype(jnp.float32) * (XATTN_HEAD_DIM ** -0.5)
    probs = jax.nn.softmax(scores, axis=-1).astype(h.dtype)
    o = jnp.einsum('bhsm,bmhd->bshd', probs, vmem).reshape(b, s, d)
    return o @ w_xo


def _moe(h, w_router, b_router, w_gu, b_gu, w_down, b_down):
    b, s, d = h.shape
    t = b * s
    hf = h.reshape(t, d)
    logits = (hf @ w_router).astype(jnp.float32) + b_router
    top_val, top_idx = lax.top_k(logits, TOP_K)
    top_w = jax.nn.softmax(top_val, axis=-1)
    n_assign = t * TOP_K
    expert_flat = top_idx.reshape(-1)
    token_flat = jnp.repeat(jnp.arange(t, dtype=jnp.int32), TOP_K)
    weight_flat = top_w.reshape(-1)
    order = jnp.argsort(expert_flat)
    e_sorted = expert_flat[order]
    counts = jnp.bincount(expert_flat, length=N_EXPERTS)
    padded = (counts + MOE_BLOCK - 1) // MOE_BLOCK * MOE_BLOCK
    start = jnp.cumsum(counts) - counts
    pstart = jnp.cumsum(padded) - padded
    dest = pstart[e_sorted] + jnp.arange(n_assign, dtype=jnp.int32) - start[e_sorted]
    n_blocks = -(-n_assign // MOE_BLOCK) + N_EXPERTS
    n_slots = n_blocks * MOE_BLOCK
    slot_tok = jnp.full((n_slots,), t, jnp.int32).at[dest].set(token_flat[order])
    slot_w = jnp.zeros((n_slots,), jnp.float32).at[dest].set(weight_flat[order])
    block_e = jnp.minimum(jnp.searchsorted(jnp.cumsum(padded), jnp.arange(n_blocks) * MOE_BLOCK, side='right'),
                          N_EXPERTS - 1)
    hf_pad = jnp.concatenate([hf, jnp.zeros((1, d), hf.dtype)], 0)

    def run_block(args):
        tok, wt, e = args
        xe = hf_pad[tok]
        gu = xe @ w_gu[e] + b_gu[e]
        gate, up = jnp.split(gu, 2, axis=-1)
        gate = jnp.minimum(gate, SWIGLU_LIMIT)
        up = jnp.clip(up, -SWIGLU_LIMIT, SWIGLU_LIMIT)
        act = (up + 1.0) * gate * jax.nn.sigmoid(gate * SWIGLU_ALPHA)
        ye = act @ w_down[e] + b_down[e]
        return ye * wt[:, None].astype(ye.dtype)

    yb = lax.map(run_block, (slot_tok.reshape(n_blocks, MOE_BLOCK), slot_w.reshape(n_blocks, MOE_BLOCK), block_e))
    out = jnp.zeros((t + 1, d), h.dtype).at[slot_tok].add(yb.reshape(n_slots, d).astype(h.dtype))
    return out[:t].reshape(b, s, d)


def setup_inputs(seed: int = 0) -> dict:
    key = jax.random.key(seed)
    ks = jax.random.split(key, 40)
    f32 = jnp.float32
    L = DEPTH

    def nrm(k, shape, scale):
        return jax.random.normal(k, shape, f32) * scale

    conv_ch = SSD_D_INNER + 2 * SSD_BC
    dt0 = jnp.exp(jax.random.uniform(ks[10], (L, 2, SSD_HEADS), f32) * (math.log(0.1) - math.log(0.001))
                  + math.log(0.001))
    dt_bias = dt0 + jnp.log(-jnp.expm1(-dt0))
    a_log = jnp.log(jax.random.uniform(ks[11], (L, 2, SSD_HEADS), f32, minval=1.0, maxval=16.0))
    positions = (jnp.arange(SEQ, dtype=jnp.int32)[None, :]
                 + jax.random.randint(ks[2], (BATCH, 1), 0, 1024, dtype=jnp.int32))
    return {
        'x': nrm(ks[0], (BATCH, SEQ, D_MODEL), 1.0),
        'mem': nrm(ks[1], (BATCH, N_MEM, D_MODEL), 1.0),
        'positions': positions,
        'ln_in_g': 1.0 + nrm(ks[3], (D_MODEL,), 0.02),
        'ln_in_b': nrm(ks[4], (D_MODEL,), 0.02),
        'ln_mem_g': 1.0 + nrm(ks[5], (D_MODEL,), 0.02),
        'ln_mem_b': nrm(ks[6], (D_MODEL,), 0.02),
        'w_in': nrm(ks[7], (L, D_MODEL, IN_WIDTH), D_MODEL ** -0.5),
        'conv_w': nrm(ks[8], (L, SSD_CONV, conv_ch), SSD_CONV ** -0.5),
        'conv_b': nrm(ks[9], (L, conv_ch), 0.02),
        'dt_bias': dt_bias,
        'a_log': a_log,
        'd_skip': 1.0 + nrm(ks[12], (L, SSD_HEADS), 0.1),
        'ssd_norm_g': 1.0 + nrm(ks[13], (L, SSD_D_INNER), 0.02),
        'w_ret_o': nrm(ks[14], (L, RET_V, D_MODEL), RET_V ** -0.5 * DEEPNORM_BETA),
        'w_ssd_o': nrm(ks[15], (L, SSD_D_INNER, D_MODEL), SSD_D_INNER ** -0.5 * DEEPNORM_BETA),
        'w_mix_o': nrm(ks[16], (L, D_MODEL, D_MODEL), D_MODEL ** -0.5 * DEEPNORM_BETA),
        'ln1_g': 1.0 + nrm(ks[17], (L, D_MODEL), 0.02),
        'ln1_b': nrm(ks[18], (L, D_MODEL), 0.02),
        'w_xq': nrm(ks[19], (L, D_MODEL, D_MODEL), D_MODEL ** -0.5),
        'w_xkv': nrm(ks[20], (L, D_MODEL, 2 * D_MODEL), D_MODEL ** -0.5),
        'w_xo': nrm(ks[21], (L, D_MODEL, D_MODEL), D_MODEL ** -0.5 * DEEPNORM_BETA),
        'ln2_g': 1.0 + nrm(ks[22], (L, D_MODEL), 0.02),
        'ln2_b': nrm(ks[23], (L, D_MODEL), 0.02),
        'w_router': nrm(ks[24], (L, D_MODEL, N_EXPERTS), D_MODEL ** -0.5),
        'b_router': nrm(ks[25], (L, N_EXPERTS), 0.01),
        'w_gu': nrm(ks[26], (L, N_EXPERTS, D_MODEL, 2 * EXPERT_DIM), D_MODEL ** -0.5),
        'b_gu': nrm(ks[27], (L, N_EXPERTS, 2 * EXPERT_DIM), 0.01),
        'w_down': nrm(ks[28], (L, N_EXPERTS, EXPERT_DIM, D_MODEL), EXPERT_DIM ** -0.5 * DEEPNORM_BETA),
        'b_down': nrm(ks[29], (L, N_EXPERTS, D_MODEL), 0.01),
        'ln3_g': 1.0 + nrm(ks[30], (L, D_MODEL), 0.02),
        'ln3_b': nrm(ks[31], (L, D_MODEL), 0.02),
    }


def reference(x, mem, positions, ln_in_g, ln_in_b, ln_mem_g, ln_mem_b, w_in, conv_w, conv_b,
              dt_bias, a_log, d_skip, ssd_norm_g, w_ret_o, w_ssd_o, w_mix_o, ln1_g, ln1_b,
              w_xq, w_xkv, w_xo, ln2_g, ln2_b, w_router, b_router, w_gu, b_gu, w_down, b_down,
              ln3_g, ln3_b):
    h = _layernorm(x, ln_in_g, ln_in_b)
    mem_n = _layernorm(mem, ln_mem_g, ln_mem_b)
    for l in range(DEPTH):
        mix = _hybrid_mixer(h, positions, w_in[l], conv_w[l], conv_b[l], dt_bias[l], a_log[l], d_skip[l],
                            ssd_norm_g[l], w_ret_o[l], w_ssd_o[l], w_mix_o[l])
        h = _layernorm(DEEPNORM_ALPHA * h + mix, ln1_g[l], ln1_b[l])
        xa = _memory_xattn(h, mem_n, w_xq[l], w_xkv[l], w_xo[l])
        h = _layernorm(DEEPNORM_ALPHA * h + xa, ln2_g[l], ln2_b[l])
        ff = _moe(h, w_router[l], b_router[l], w_gu[l], b_gu[l], w_down[l], b_down[l])
        h = _layernorm(DEEPNORM_ALPHA * h + ff, ln3_g[l], ln3_b[l])
    return h
```

```python
import functools

import jax
import jax.numpy as jnp
from jax import lax
from jax.experimental import pallas as pl
from jax.experimental.pallas import tpu as pltpu

F32 = jnp.float32
BF16 = jnp.bfloat16

D_MODEL = 2048
N_MEM = 256
RET_HEADS = 8
RET_QK_DIM = 128
RET_V_DIM = 256
RET_Q = RET_HEADS * RET_QK_DIM
RET_V = RET_HEADS * RET_V_DIM
RET_CHUNK = 128
ROPE_BASE = 10000.0
SSD_D_INNER = 2 * D_MODEL
SSD_HEAD_DIM = 64
SSD_HEADS = SSD_D_INNER // SSD_HEAD_DIM
SSD_GROUPS = 8
SSD_HPG = SSD_HEADS // SSD_GROUPS
SSD_GROUP_W = SSD_D_INNER // SSD_GROUPS
SSD_STATE = 128
SSD_BC = SSD_GROUPS * SSD_STATE
SSD_CONV = 5
SSD_SUB = 64
CONV_CH = SSD_D_INNER + 2 * SSD_BC
XATTN_HEADS = 4
XATTN_HEAD_DIM = D_MODEL // XATTN_HEADS
N_EXPERTS = 32
TOP_K = 4
EXPERT_DIM = D_MODEL // 2
SWIGLU_LIMIT = 7.0
SWIGLU_ALPHA = 1.702
MOE_BLOCK = 256
LN_EPS = 1e-5
NEG_BIG = -1e30

OFF_Q = 0
OFF_K = OFF_Q + RET_Q
OFF_V = OFF_K + RET_Q
OFF_GRET = OFF_V + RET_V
OFF_Z = OFF_GRET + RET_V
OFF_XS = OFF_Z + SSD_D_INNER
OFF_BM = OFF_XS + SSD_D_INNER
OFF_CM = OFF_BM + SSD_BC
OFF_DT = OFF_CM + SSD_BC
OFF_GATE_R = OFF_DT + 2 * SSD_HEADS
OFF_GATE_S = OFF_GATE_R + D_MODEL
IN_WIDTH = OFF_GATE_S + D_MODEL
MAIN_W = IN_WIDTH - 2 * SSD_HEADS
M_GATE_R = OFF_DT
M_GATE_S = OFF_DT + D_MODEL

VMEM_LIMIT_MB = 48


def _cparams(sem, vmem_mb=VMEM_LIMIT_MB):
    return pltpu.CompilerParams(dimension_semantics=sem, vmem_limit_bytes=vmem_mb * 1024 * 1024)


def _dot(a, b):
    return jnp.dot(a, b, preferred_element_type=F32)


def _dot_nt(a, b):
    return lax.dot_general(a, b, (((1,), (1,)), ((), ())), preferred_element_type=F32)


def _dot_tn(a, b):
    return lax.dot_general(a, b, (((0,), (0,)), ((), ())), preferred_element_type=F32)


def _sigmoid(x):
    return 1.0 / (1.0 + jnp.exp(-x))


def _softplus(x):
    return jnp.maximum(x, 0.0) + jnp.log(1.0 + jnp.exp(-jnp.abs(x)))


def _split3(a):
    a1 = a.astype(BF16)
    r1 = a - a1.astype(F32)
    a2 = r1.astype(BF16)
    a3 = (r1 - a2.astype(F32)).astype(BF16)
    return a1, a2, a3


def _ln_rows(x, g, b):
    mu = jnp.mean(x, axis=-1, keepdims=True)
    xc = x - mu
    var = jnp.mean(xc * xc, axis=-1, keepdims=True)
    return xc * lax.rsqrt(var + LN_EPS) * g + b


def _ln_kernel(x_ref, g_ref, b_ref, o_ref, ob_ref):
    y = _ln_rows(x_ref[...], g_ref[...], b_ref[...])
    o_ref[...] = y
    ob_ref[...] = y.astype(BF16)


def _layernorm(x, g, b):
    m, d = x.shape
    tm = min(512, m)
    return pl.pallas_call(
        _ln_kernel,
        grid=(m // tm,),
        in_specs=[pl.BlockSpec((tm, d), lambda i: (i, 0)),
                  pl.BlockSpec((1, d), lambda i: (0, 0)),
                  pl.BlockSpec((1, d), lambda i: (0, 0))],
        out_specs=[pl.BlockSpec((tm, d), lambda i: (i, 0)),
                   pl.BlockSpec((tm, d), lambda i: (i, 0))],
        out_shape=[jax.ShapeDtypeStruct((m, d), F32), jax.ShapeDtypeStruct((m, d), BF16)],
        compiler_params=_cparams(("parallel",)),
        name="layernorm",
    )(x, g.reshape(1, d), b.reshape(1, d))


def _mm_kernel(x_ref, w_ref, o_ref):
    o_ref[...] = _dot(x_ref[...], w_ref[...]).astype(o_ref.dtype)


def _matmul(x, w, out_dtype, tm, tn, name):
    m, k = x.shape
    n = w.shape[1]
    tm = min(tm, m)
    tn = min(tn, n)
    return pl.pallas_call(
        _mm_kernel,
        grid=(m // tm, n // tn),
        in_specs=[pl.BlockSpec((tm, k), lambda i, j: (i, 0)),
                  pl.BlockSpec((k, tn), lambda i, j: (0, j))],
        out_specs=pl.BlockSpec((tm, tn), lambda i, j: (i, j)),
        out_shape=jax.ShapeDtypeStruct((m, n), out_dtype),
        compiler_params=_cparams(("parallel", "arbitrary")),
        name=name,
    )(x, w)


def _rope_kernel(pos_ref, inv_ref, sgn_ref, cos_ref, sin_ref):
    ang = pos_ref[...] * inv_ref[...]
    cos_ref[...] = jnp.cos(ang)
    sin_ref[...] = jnp.sin(ang) * sgn_ref[...]


def _rope_tables(positions):
    t = positions.size
    half = RET_QK_DIM // 2
    inv = ROPE_BASE ** (-jnp.arange(half, dtype=F32) / half)
    inv2 = jnp.concatenate([inv, inv]).reshape(1, RET_QK_DIM)
    sgn = jnp.concatenate([-jnp.ones((half,), F32), jnp.ones((half,), F32)]).reshape(1, RET_QK_DIM)
    pos = positions.astype(F32).reshape(t, 1)
    tm = min(1024, t)
    return pl.pallas_call(
        _rope_kernel,
        grid=(t // tm,),
        in_specs=[pl.BlockSpec((tm, 1), lambda i: (i, 0)),
                  pl.BlockSpec((1, RET_QK_DIM), lambda i: (0, 0)),
                  pl.BlockSpec((1, RET_QK_DIM), lambda i: (0, 0))],
        out_specs=[pl.BlockSpec((tm, RET_QK_DIM), lambda i: (i, 0)),
                   pl.BlockSpec((tm, RET_QK_DIM), lambda i: (i, 0))],
        out_shape=[jax.ShapeDtypeStruct((t, RET_QK_DIM), F32)] * 2,
        compiler_params=_cparams(("parallel",)),
        name="rope_tables",
    )(pos, inv2, sgn)


def _ret_kernel(*refs, reverse, nchunk):
    if not reverse:
        (q_ref, k_ref, v_ref, cos_ref, sin_ref, qd_ref, kd_ref, cd_ref, dm_ref, y_ref, st_ref) = refs
    else:
        (q_ref, k_ref, v_ref, cos_ref, sin_ref, qd_ref, kd_ref, cd_ref, yf_ref, g_ref, o_ref, st_ref) = refs

    @pl.when(pl.program_id(2) == 0)
    def _():
        st_ref[...] = jnp.zeros_like(st_ref)

    qd = qd_ref[...]
    kd = kd_ref[...]
    cd = cd_ref[...]
    scale = RET_QK_DIM ** -0.5
    c = RET_CHUNK
    order = range(nchunk - 1, -1, -1) if reverse else range(nchunk)
    for ci in order:
        sl = slice(ci * c, (ci + 1) * c)
        cs = cos_ref[sl, :]
        sn = sin_ref[sl, :]
        q = q_ref[sl, :].astype(F32)
        k = k_ref[sl, :].astype(F32)
        v = v_ref[sl, :]
        qr = q * cs + pltpu.roll(q, RET_QK_DIM // 2, 1) * sn
        kr = (k * cs + pltpu.roll(k, RET_QK_DIM // 2, 1) * sn) * scale
        st = st_ref[...]
        y = _dot((qr * qd).astype(BF16), st.astype(BF16))
        st_ref[...] = cd * st + _dot_tn((kr * kd).astype(BF16), v)
        if not reverse:
            s = _dot_nt(qr.astype(BF16), kr.astype(BF16)) * dm_ref[...]
            y_ref[sl, :] = y + _dot(s.astype(BF16), v)
        else:
            tot = yf_ref[sl, :] + y
            mu = jnp.mean(tot, axis=-1, keepdims=True)
            tc = tot - mu
            var = jnp.mean(tc * tc, axis=-1, keepdims=True)
            g = g_ref[sl, :].astype(F32)
            o_ref[sl, :] = (tc * lax.rsqrt(var + LN_EPS) * (g * _sigmoid(g))).astype(BF16)


def _ret_decay_tables():
    lg = jnp.log1p(-jnp.exp2(-5.0 - jnp.arange(RET_HEADS, dtype=F32)))
    idx = jnp.arange(RET_CHUNK, dtype=F32)
    c = float(RET_CHUNK)
    dist = jnp.abs(idx[:, None] - idx[None, :])
    dmat = jnp.exp(lg[:, None, None] * dist)

    def rows(e):
        return jnp.broadcast_to(jnp.exp(lg[:, None] * e)[..., None], (RET_HEADS, RET_CHUNK, RET_QK_DIM))

    qd_f = rows(idx + 1.0)
    kd_f = rows(c - 1.0 - idx)
    qd_b = rows(c - idx)
    kd_b = rows(idx)
    cd = jnp.broadcast_to(jnp.exp(lg * c)[:, None, None], (RET_HEADS, 1, RET_V_DIM))
    return dmat, (qd_f, kd_f), (qd_b, kd_b), cd


def _retention(proj3, cos, sin):
    b, s, _ = proj3.shape
    rb = min(512, s)
    nb = s // rb
    nchunk = rb // RET_CHUNK
    dmat, dec_f, dec_b, cd = _ret_decay_tables()
    cos3 = cos.reshape(b, s, RET_QK_DIM)
    sin3 = sin.reshape(b, s, RET_QK_DIM)
    kq = OFF_K // RET_QK_DIM
    kv = OFF_V // RET_V_DIM
    kg = OFF_GRET // RET_V_DIM
    tab = lambda w: pl.BlockSpec((None, RET_CHUNK, w), lambda bi, h, i: (h, 0, 0))
    cd_spec = pl.BlockSpec((None, 1, RET_V_DIM), lambda bi, h, i: (h, 0, 0))

    def specs(rev):
        blk = (lambda i: nb - 1 - i) if rev else (lambda i: i)
        return [
            pl.BlockSpec((None, rb, RET_QK_DIM), lambda bi, h, i: (bi, blk(i), h)),
            pl.BlockSpec((None, rb, RET_QK_DIM), lambda bi, h, i: (bi, blk(i), kq + h)),
            pl.BlockSpec((None, rb, RET_V_DIM), lambda bi, h, i: (bi, blk(i), kv + h)),
            pl.BlockSpec((None, rb, RET_QK_DIM), lambda bi, h, i: (bi, blk(i), 0)),
            pl.BlockSpec((None, rb, RET_QK_DIM), lambda bi, h, i: (bi, blk(i), 0)),
            tab(RET_QK_DIM), tab(RET_QK_DIM), cd_spec,
        ], blk

    sem = ("parallel", "parallel", "arbitrary")
    scratch = [pltpu.VMEM((RET_QK_DIM, RET_V_DIM), F32)]
    in_f, _ = specs(False)
    y_f = pl.pallas_call(
        functools.partial(_ret_kernel, reverse=False, nchunk=nchunk),
        grid=(b, RET_HEADS, nb),
        in_specs=in_f + [tab(RET_CHUNK)],
        out_specs=pl.BlockSpec((None, rb, RET_V_DIM), lambda bi, h, i: (bi, i, h)),
        out_shape=jax.ShapeDtypeStruct((b, s, RET_V), F32),
        scratch_shapes=scratch,
        compiler_params=_cparams(sem),
        name="retention_fwd",
    )(proj3, proj3, proj3, cos3, sin3, dec_f[0], dec_f[1], cd, dmat)
    in_b, blk = specs(True)
    out = pl.pallas_call(
        functools.partial(_ret_kernel, reverse=True, nchunk=nchunk),
        grid=(b, RET_HEADS, nb),
        in_specs=in_b + [
            pl.BlockSpec((None, rb, RET_V_DIM), lambda bi, h, i: (bi, blk(i), h)),
            pl.BlockSpec((None, rb, RET_V_DIM), lambda bi, h, i: (bi, blk(i), kg + h)),
        ],
        out_specs=pl.BlockSpec((None, rb, RET_V_DIM), lambda bi, h, i: (bi, blk(i), h)),
        out_shape=jax.ShapeDtypeStruct((b, s, RET_V), BF16),
        scratch_shapes=scratch,
        compiler_params=_cparams(sem),
        name="retention_bwd",
    )(proj3, proj3, proj3, cos3, sin3, dec_b[0], dec_b[1], cd, y_f, proj3)
    return out


CONV_HALO = 16


def _conv_kernel(xm_ref, xp_ref, xn_ref, w_ref, b_ref, o_ref, ext_ref):
    i = pl.program_id(1)
    n = pl.num_programs(1)
    tm = xm_ref.shape[0]
    pad = SSD_CONV // 2
    ext_ref[0:CONV_HALO, :] = jnp.where(i == 0, 0.0, xp_ref[...].astype(F32))
    ext_ref[CONV_HALO:CONV_HALO + tm, :] = xm_ref[...].astype(F32)
    ext_ref[CONV_HALO + tm:2 * CONV_HALO + tm, :] = jnp.where(i == n - 1, 0.0, xn_ref[...].astype(F32))
    acc = b_ref[...] + w_ref[0:1, :] * ext_ref[CONV_HALO - pad:CONV_HALO - pad + tm, :]
    for k in range(1, SSD_CONV):
        o = CONV_HALO - pad + k
        acc = acc + w_ref[k:k + 1, :] * ext_ref[o:o + tm, :]
    o_ref[...] = (acc * _sigmoid(acc)).astype(BF16)


def _conv_silu(proj3, conv_w, conv_b):
    b, s, _ = proj3.shape
    tm = min(512, s)
    tc = 512
    ns = s // tm
    c0 = OFF_XS // tc
    hb = tm // CONV_HALO
    last = s // CONV_HALO - 1
    return pl.pallas_call(
        _conv_kernel,
        grid=(b, ns, CONV_CH // tc),
        in_specs=[
            pl.BlockSpec((None, tm, tc), lambda bi, i, j: (bi, i, c0 + j)),
            pl.BlockSpec((None, CONV_HALO, tc), lambda bi, i, j: (bi, jnp.maximum(i * hb - 1, 0), c0 + j)),
            pl.BlockSpec((None, CONV_HALO, tc), lambda bi, i, j: (bi, jnp.minimum((i + 1) * hb, last), c0 + j)),
            pl.BlockSpec((SSD_CONV, tc), lambda bi, i, j: (0, j)),
            pl.BlockSpec((1, tc), lambda bi, i, j: (0, j)),
        ],
        out_specs=pl.BlockSpec((None, tm, tc), lambda bi, i, j: (bi, i, j)),
        out_shape=jax.ShapeDtypeStruct((b, s, CONV_CH), BF16),
        scratch_shapes=[pltpu.VMEM((tm + 2 * CONV_HALO, tc), F32)],
        compiler_params=_cparams(("parallel", "parallel", "parallel")),
        name="conv_silu",
    )(proj3, proj3, proj3, conv_w, conv_b.reshape(1, CONV_CH))


def _ssd_kernel(*refs, reverse, nsuper):
    if not reverse:
        (x_ref, b_ref, c_ref, dt_ref, bw_ref, aw_ref, bn_ref, an_ref, y_ref, h_ref) = refs
    else:
        (x_ref, b_ref, c_ref, dt_ref, bw_ref, aw_ref, bn_ref, an_ref,
         yf_ref, z_ref, dsk_ref, ng_ref, o_ref, h_ref) = refs

    @pl.when(pl.program_id(2) == 0)
    def _():
        h_ref[...] = jnp.zeros_like(h_ref)

    g = pl.program_id(1)
    sub = SSD_SUB
    sup = 2 * sub
    gw = SSD_GROUP_W
    base = (SSD_HEADS if reverse else 0) + g * SSD_HPG
    i32 = jnp.int32
    ci = lax.broadcasted_iota(i32, (sup, gw), 0)
    li = lax.broadcasted_iota(i32, (sup, gw), 1)
    selw = (ci == base + (li >> 6)).astype(BF16)
    rn = lax.broadcasted_iota(i32, (16, sup), 0)
    cn = lax.broadcasted_iota(i32, (16, sup), 1)
    seln = jnp.logical_and(cn == base + rn, rn < SSD_HPG).astype(BF16)
    ii = lax.broadcasted_iota(i32, (sup, sup), 0)
    jj = lax.broadcasted_iota(i32, (sup, sup), 1)
    same = (ii >> 6) == (jj >> 6)
    tri = jnp.logical_and(same, (jj >= ii) if reverse else (jj <= ii)).astype(BF16)
    i2 = lax.broadcasted_iota(i32, (sub, sup), 0)
    j2 = lax.broadcasted_iota(i32, (sub, sup), 1) & (sub - 1)
    lmask = (j2 > i2) if reverse else (i2 >= j2)
    bw = bw_ref[...]
    aw = -jnp.exp(aw_ref[...])
    bn = bn_ref[...]
    an = -jnp.exp(an_ref[...])

    def body(t, carry):
        sc = (nsuper - 1 - t) if reverse else t
        r0 = pl.multiple_of(sc * sup, sup)
        rows = pl.ds(r0, sup)
        raw = dt_ref[rows, :]
        rs = _split3(raw)
        dtw = _softplus(sum(_dot(r, selw) for r in rs) + bw)
        pw = sum(_dot(tri, d) for d in _split3(dtw * aw))
        dtn = _softplus(sum(_dot_nt(seln, r) for r in rs) + bn)
        pn = sum(_dot_nt(d, tri) for d in _split3(dtn * an))
        x = x_ref[rows, :].astype(F32)
        xdt = x * dtw
        bm = b_ref[rows, :]
        cm = c_ref[rows, :]
        zeros_half = jnp.zeros((sub, gw), BF16)
        ys = [None, None]
        for s in ((1, 0) if reverse else (0, 1)):
            lo = s * sub
            cms = cm[lo:lo + sub, :]
            bms = bm[lo:lo + sub, :]
            cb2 = _dot_nt(cms, jnp.concatenate([bms, bms], axis=0))
            pws = pw[lo:lo + sub, :]
            xds = xdt[lo:lo + sub, :]
            parts = []
            for p in range(SSD_HPG // 2):
                colp = pws[:, p * sup:(p + 1) * sup]
                rowp = jnp.concatenate([pn[2 * p:2 * p + 1, lo:lo + sub],
                                        pn[2 * p + 1:2 * p + 2, lo:lo + sub]], axis=1)
                lm = jnp.exp(jnp.where(lmask, colp - rowp, NEG_BIG))
                w = (cb2 * lm).astype(BF16)
                xp = xds[:, p * sup:(p + 1) * sup].astype(BF16)
                z64 = jnp.zeros((sub, sub), BF16)
                xblk = jnp.concatenate(
                    [jnp.concatenate([xp[:, :sub], z64], axis=1),
                     jnp.concatenate([z64, xp[:, sub:]], axis=1)], axis=0)
                parts.append(_dot(w, xblk))
            hst = h_ref[...]
            y = jnp.concatenate(parts, axis=1) + _dot(cms, hst.astype(BF16)) * jnp.exp(pws)
            plast = pws[0:1, :] if reverse else pws[sub - 1:sub, :]
            xdec = (jnp.exp(plast - pws) * xds).astype(BF16)
            xfull = jnp.concatenate([xdec, zeros_half] if s == 0 else [zeros_half, xdec], axis=0)
            h_ref[...] = jnp.exp(plast) * hst + _dot_tn(bm, xfull)
            ys[s] = y
        y = jnp.concatenate(ys, axis=0)
        if not reverse:
            y_ref[rows, :] = y
        else:
            tot = yf_ref[rows, :] + y + dsk_ref[...] * x
            z = z_ref[rows, :].astype(F32)
            tot = tot * (z * _sigmoid(z))
            ms = jnp.mean(tot * tot, axis=-1, keepdims=True)
            o_ref[rows, :] = (tot * lax.rsqrt(ms + LN_EPS) * ng_ref[...]).astype(BF16)
        return carry

    lax.fori_loop(0, nsuper, body, 0)


def _ssd(xbc, proj3, dtraw3, dt_bias, a_log, d_skip, norm_g):
    b, s, _ = xbc.shape
    rb = min(512, s)
    nb = s // rb
    nsuper = rb // (2 * SSD_SUB)
    gw = SSD_GROUP_W
    hp = SSD_HPG

    def wide(p):
        return jnp.broadcast_to(p.astype(F32)[:, :, None], (2, SSD_HEADS, SSD_HEAD_DIM)).reshape(2 * SSD_GROUPS, 1, gw)

    def narrow(p):
        q = p.astype(F32).reshape(2 * SSD_GROUPS, hp, 1)
        q = jnp.concatenate([q, jnp.zeros((2 * SSD_GROUPS, 16 - hp, 1), F32)], axis=1)
        return jnp.broadcast_to(q, (2 * SSD_GROUPS, 16, 2 * SSD_SUB))

    bw, aw, bn, an = wide(dt_bias), wide(a_log), narrow(dt_bias), narrow(a_log)
    dsk = jnp.broadcast_to(d_skip.astype(F32)[:, None], (SSD_HEADS, SSD_HEAD_DIM)).reshape(SSD_GROUPS, 1, gw)
    ng = norm_g.astype(F32).reshape(SSD_GROUPS, 1, gw)
    kb = SSD_D_INNER // SSD_STATE
    kc = (SSD_D_INNER + SSD_BC) // SSD_STATE
    kz = OFF_Z // gw

    def specs(rev):
        blk = (lambda i: nb - 1 - i) if rev else (lambda i: i)
        d = SSD_GROUPS if rev else 0
        small_w = pl.BlockSpec((None, 1, gw), lambda bi, g, i: (d + g, 0, 0))
        small_n = pl.BlockSpec((None, 16, 2 * SSD_SUB), lambda bi, g, i: (d + g, 0, 0))
        return [
            pl.BlockSpec((None, rb, gw), lambda bi, g, i: (bi, blk(i), g)),
            pl.BlockSpec((None, rb, SSD_STATE), lambda bi, g, i: (bi, blk(i), kb + g)),
            pl.BlockSpec((None, rb, SSD_STATE), lambda bi, g, i: (bi, blk(i), kc + g)),
            pl.BlockSpec((None, rb, 2 * SSD_HEADS), lambda bi, g, i: (bi, blk(i), 0)),
            small_w, small_w, small_n, small_n,
        ], blk

    sem = ("parallel", "parallel", "arbitrary")
    scratch = [pltpu.VMEM((SSD_STATE, gw), F32)]
    in_f, _ = specs(False)
    y_f = pl.pallas_call(
        functools.partial(_ssd_kernel, reverse=False, nsuper=nsuper),
        grid=(b, SSD_GROUPS, nb),
        in_specs=in_f,
        out_specs=pl.BlockSpec((None, rb, gw), lambda bi, g, i: (bi, i, g)),
        out_shape=jax.ShapeDtypeStruct((b, s, SSD_D_INNER), F32),
        scratch_shapes=scratch,
        compiler_params=_cparams(sem),
        name="ssd_fwd",
    )(xbc, xbc, xbc, dtraw3, bw, aw, bn, an)
    in_b, blk = specs(True)
    grp = pl.BlockSpec((None, 1, gw), lambda bi, g, i: (g, 0, 0))
    out = pl.pallas_call(
        functools.partial(_ssd_kernel, reverse=True, nsuper=nsuper),
        grid=(b, SSD_GROUPS, nb),
        in_specs=in_b + [
            pl.BlockSpec((None, rb, gw), lambda bi, g, i: (bi, blk(i), g)),
            pl.BlockSpec((None, rb, gw), lambda bi, g, i: (bi, blk(i), kz + g)),
            grp, grp,
        ],
        out_specs=pl.BlockSpec((None, rb, gw), lambda bi, g, i: (bi, blk(i), g)),
        out_shape=jax.ShapeDtypeStruct((b, s, SSD_D_INNER), BF16),
        scratch_shapes=scratch,
        compiler_params=_cparams(sem),
        name="ssd_bwd",
    )(xbc, xbc, xbc, dtraw3, bw, aw, bn, an, y_f, proj3, dsk, ng)
    return out


def _merge_kernel(ret_ref, ssd_ref, wr_ref, ws_ref, gr_ref, gs_ref, o_ref):
    yr = _dot(ret_ref[...], wr_ref[...])
    ys = _dot(ssd_ref[...], ws_ref[...])
    o = _sigmoid(gr_ref[...].astype(F32)) * yr + _sigmoid(gs_ref[...].astype(F32)) * ys
    o_ref[...] = o.astype(BF16)


def _merge(ret, ssd, proj, w_ret_o, w_ssd_o):
    m = ret.shape[0]
    tm = min(1024, m)
    tn = 256
    gr0 = M_GATE_R // tn
    gs0 = M_GATE_S // tn
    return pl.pallas_call(
        _merge_kernel,
        grid=(m // tm, D_MODEL // tn),
        in_specs=[
            pl.BlockSpec((tm, RET_V), lambda i, j: (i, 0)),
            pl.BlockSpec((tm, SSD_D_INNER), lambda i, j: (i, 0)),
            pl.BlockSpec((RET_V, tn), lambda i, j: (0, j)),
            pl.BlockSpec((SSD_D_INNER, tn), lambda i, j: (0, j)),
            pl.BlockSpec((tm, tn), lambda i, j: (i, gr0 + j)),
            pl.BlockSpec((tm, tn), lambda i, j: (i, gs0 + j)),
        ],
        out_specs=pl.BlockSpec((tm, tn), lambda i, j: (i, j)),
        out_shape=jax.ShapeDtypeStruct((m, D_MODEL), BF16),
        compiler_params=_cparams(("parallel", "arbitrary")),
        name="branch_merge",
    )(ret, ssd, w_ret_o, w_ssd_o, proj, proj)


def _proj_ln_kernel(x_ref, w_ref, h_ref, g_ref, b_ref, o_ref, ob_ref, *, alpha):
    y = _dot(x_ref[...], w_ref[...]) + alpha * h_ref[...]
    y = _ln_rows(y, g_ref[...], b_ref[...])
    o_ref[...] = y
    ob_ref[...] = y.astype(BF16)


def _proj_res_ln(x, w, h, g, b, alpha):
    m, k = x.shape
    d = w.shape[1]
    tm = min(256, m)
    row = lambda i: (i, 0)
    fix = lambda i: (0, 0)
    return pl.pallas_call(
        functools.partial(_proj_ln_kernel, alpha=alpha),
        grid=(m // tm,),
        in_specs=[pl.BlockSpec((tm, k), row), pl.BlockSpec((k, d), fix), pl.BlockSpec((tm, d), row),
                  pl.BlockSpec((1, d), fix), pl.BlockSpec((1, d), fix)],
        out_specs=[pl.BlockSpec((tm, d), row), pl.BlockSpec((tm, d), row)],
        out_shape=[jax.ShapeDtypeStruct((m, d), F32), jax.ShapeDtypeStruct((m, d), BF16)],
        compiler_params=_cparams(("parallel",)),
        name="proj_residual_ln",
    )(x, w, h, g.reshape(1, d), b.reshape(1, d))


def _xattn_kernel(q_ref, k_ref, v_ref, wo_ref, h_ref, g_ref, b_ref, o_ref, ob_ref, *, alpha):
    scale = XATTN_HEAD_DIM ** -0.5
    outs = []
    for hd in range(XATTN_HEADS):
        sl = slice(hd * XATTN_HEAD_DIM, (hd + 1) * XATTN_HEAD_DIM)
        s = _dot_nt(q_ref[:, sl], k_ref[:, sl]) * scale
        e = jnp.exp(s - jnp.max(s, axis=-1, keepdims=True))
        p = e / jnp.sum(e, axis=-1, keepdims=True)
        outs.append(_dot(p.astype(BF16), v_ref[:, sl]).astype(BF16))
    o = jnp.concatenate(outs, axis=1)
    y = _dot(o, wo_ref[...]) + alpha * h_ref[...]
    y = _ln_rows(y, g_ref[...], b_ref[...])
    o_ref[...] = y
    ob_ref[...] = y.astype(BF16)


def _xattn(q3, kv3, w_xo, h3, g, b, alpha):
    bsz, s, d = q3.shape
    tm = min(256, s)
    row = lambda bi, i: (bi, i, 0)
    fix = lambda bi, i: (0, 0)
    return pl.pallas_call(
        functools.partial(_xattn_kernel, alpha=alpha),
        grid=(bsz, s // tm),
        in_specs=[
            pl.BlockSpec((None, tm, d), row),
            pl.BlockSpec((None, N_MEM, d), lambda bi, i: (bi, 0, 0)),
            pl.BlockSpec((None, N_MEM, d), lambda bi, i: (bi, 0, 1)),
            pl.BlockSpec((d, d), fix),
            pl.BlockSpec((None, tm, d), row),
            pl.BlockSpec((1, d), fix), pl.BlockSpec((1, d), fix),
        ],
        out_specs=[pl.BlockSpec((None, tm, d), row), pl.BlockSpec((None, tm, d), row)],
        out_shape=[jax.ShapeDtypeStruct((bsz, s, d), F32), jax.ShapeDtypeStruct((bsz, s, d), BF16)],
        compiler_params=_cparams(("parallel", "parallel")),
        name="memory_xattn",
    )(q3, kv3, kv3, w_xo, h3, g.reshape(1, d), b.reshape(1, d))


ROUTE_LANES = 128


def _router_kernel(h_ref, w_ref, b_ref, idx_ref, wt_ref):
    lg = jnp.dot(h_ref[...], w_ref[...], preferred_element_type=F32,
                 precision=lax.Precision.HIGHEST) + b_ref[...]
    tm = lg.shape[0]
    lane = lax.broadcasted_iota(jnp.int32, (tm, N_EXPERTS), 1).astype(F32)
    vals, idxs = [], []
    for _ in range(TOP_K):
        m = jnp.max(lg, axis=-1, keepdims=True)
        am = jnp.min(jnp.where(lg == m, lane, float(N_EXPERTS)), axis=-1, keepdims=True)
        vals.append(m)
        idxs.append(am.astype(jnp.int32))
        lg = jnp.where(lane == am, -jnp.inf, lg)
    es = [jnp.exp(v - vals[0]) for v in vals]
    tot = es[0] + es[1] + es[2] + es[3]
    out_lane = lax.broadcasted_iota(jnp.int32, (tm, ROUTE_LANES), 1)
    io = jnp.zeros((tm, ROUTE_LANES), jnp.int32)
    wo = jnp.zeros((tm, ROUTE_LANES), F32)
    for k in range(TOP_K):
        io = jnp.where(out_lane == k, idxs[k], io)
        wo = jnp.where(out_lane == k, es[k] / tot, wo)
    idx_ref[...] = io
    wt_ref[...] = wo


def _router(h, w_router, b_router):
    m, d = h.shape
    tm = min(512, m)
    row = lambda i: (i, 0)
    fix = lambda i: (0, 0)
    idx, wt = pl.pallas_call(
        _router_kernel,
        grid=(m // tm,),
        in_specs=[pl.BlockSpec((tm, d), row), pl.BlockSpec((d, N_EXPERTS), fix),
                  pl.BlockSpec((1, N_EXPERTS), fix)],
        out_specs=[pl.BlockSpec((tm, ROUTE_LANES), row), pl.BlockSpec((tm, ROUTE_LANES), row)],
        out_shape=[jax.ShapeDtypeStruct((m, ROUTE_LANES), jnp.int32),
                   jax.ShapeDtypeStruct((m, ROUTE_LANES), F32)],
        compiler_params=_cparams(("parallel",)),
        name="router_topk",
    )(h, w_router, b_router.reshape(1, N_EXPERTS))
    return idx[:, :TOP_K], wt[:, :TOP_K]


def _gather_rows(idx_ref, base, n, src_hbm, dst, sem):
    def body(r, c):
        row = idx_ref[base + r]
        pltpu.make_async_copy(src_hbm.at[pl.ds(row, 1), :], dst.at[pl.ds(r, 1), :], sem).start()
        return c
    lax.fori_loop(0, n, body, 0)


def _wait_rows(n, src_hbm, dst, sem):
    pltpu.make_async_copy(src_hbm.at[pl.ds(0, n), :], dst, sem).wait()


def _moe_kernel(be_ref, nu_ref, tok_ref, h_hbm, wgu_ref, bgu_ref, wd_ref, bd_ref, sw_ref,
                o_ref, xbuf, sem):
    i = pl.program_id(0)
    nused = nu_ref[0]
    mb = MOE_BLOCK

    @pl.when(jnp.logical_and(i == 0, nused > 0))
    def _():
        _gather_rows(tok_ref, 0, mb, h_hbm, xbuf.at[0], sem.at[0])

    @pl.when(i + 1 < nused)
    def _():
        nxt = (i + 1) % 2
        _gather_rows(tok_ref, (i + 1) * mb, mb, h_hbm, xbuf.at[nxt], sem.at[nxt])

    @pl.when(i < nused)
    def _():
        slot = i % 2
        _wait_rows(mb, h_hbm, xbuf.at[slot], sem.at[slot])
        x = xbuf[slot].astype(BF16)
        gu = _dot(x, wgu_ref[...]) + bgu_ref[...]
        gate = jnp.minimum(gu[:, :EXPERT_DIM], SWIGLU_LIMIT)
        up = jnp.clip(gu[:, EXPERT_DIM:], -SWIGLU_LIMIT, SWIGLU_LIMIT)
        act = (up + 1.0) * gate * _sigmoid(gate * SWIGLU_ALPHA)
        ye = _dot(act.astype(BF16), wd_ref[...]) + bd_ref[...]
        o_ref[...] = ye * sw_ref[...]

    @pl.when(i >= nused)
    def _():
        o_ref[...] = jnp.zeros_like(o_ref)


def _moe_experts(h, block_e, n_used, slot_tok, slot_w, w_gu, b_gu, w_down, b_down):
    t, d = h.shape
    nblk = block_e.shape[0]
    mb = MOE_BLOCK
    grid_spec = pltpu.PrefetchScalarGridSpec(
        num_scalar_prefetch=3,
        grid=(nblk,),
        in_specs=[
            pl.BlockSpec(memory_space=pl.ANY),
            pl.BlockSpec((None, d, 2 * EXPERT_DIM), lambda i, be, nu, tk: (be[i], 0, 0)),
            pl.BlockSpec((None, 1, 2 * EXPERT_DIM), lambda i, be, nu, tk: (be[i], 0, 0)),
            pl.BlockSpec((None, EXPERT_DIM, d), lambda i, be, nu, tk: (be[i], 0, 0)),
            pl.BlockSpec((None, 1, d), lambda i, be, nu, tk: (be[i], 0, 0)),
            pl.BlockSpec((mb, 1), lambda i, be, nu, tk: (i, 0)),
        ],
        out_specs=pl.BlockSpec((mb, d), lambda i, be, nu, tk: (i, 0)),
        scratch_shapes=[pltpu.VMEM((2, mb, d), F32), pltpu.SemaphoreType.DMA((2,))],
    )
    return pl.pallas_call(
        _moe_kernel,
        grid_spec=grid_spec,
        out_shape=jax.ShapeDtypeStruct((nblk * mb, d), F32),
        compiler_params=_cparams(("arbitrary",)),
        name="moe_experts",
    )(block_e, n_used, slot_tok, h, w_gu, b_gu.reshape(N_EXPERTS, 1, -1), w_down,
      b_down.reshape(N_EXPERTS, 1, -1), slot_w.reshape(nblk * mb, 1))


def _combine_kernel(dest_ref, yb_hbm, h_ref, g_ref, b_ref, o_ref, ob_ref, gbuf, sem, *, alpha, tm):
    i = pl.program_id(0)
    n = pl.num_programs(0)
    rows = TOP_K * tm

    @pl.when(i == 0)
    def _():
        _gather_rows(dest_ref, 0, rows, yb_hbm, gbuf.at[0], sem.at[0])

    @pl.when(i + 1 < n)
    def _():
        nxt = (i + 1) % 2
        _gather_rows(dest_ref, (i + 1) * rows, rows, yb_hbm, gbuf.at[nxt], sem.at[nxt])

    slot = i % 2
    _wait_rows(rows, yb_hbm, gbuf.at[slot], sem.at[slot])
    ff = gbuf[slot, 0:tm, :]
    for k in range(1, TOP_K):
        ff = ff + gbuf[slot, k * tm:(k + 1) * tm, :]
    y = _ln_rows(alpha * h_ref[...] + ff, g_ref[...], b_ref[...])
    o_ref[...] = y
    ob_ref[...] = y.astype(BF16)


def _moe_combine(yb, dest_blk, h, g, b, alpha, tm):
    t, d = h.shape
    row = lambda i, ds: (i, 0)
    fix = lambda i, ds: (0, 0)
    grid_spec = pltpu.PrefetchScalarGridSpec(
        num_scalar_prefetch=1,
        grid=(t // tm,),
        in_specs=[pl.BlockSpec(memory_space=pl.ANY), pl.BlockSpec((tm, d), row),
                  pl.BlockSpec((1, d), fix), pl.BlockSpec((1, d), fix)],
        out_specs=[pl.BlockSpec((tm, d), row), pl.BlockSpec((tm, d), row)],
        scratch_shapes=[pltpu.VMEM((2, TOP_K * tm, d), F32), pltpu.SemaphoreType.DMA((2,))],
    )
    return pl.pallas_call(
        functools.partial(_combine_kernel, alpha=alpha, tm=tm),
        grid_spec=grid_spec,
        out_shape=[jax.ShapeDtypeStruct((t, d), F32), jax.ShapeDtypeStruct((t, d), BF16)],
        compiler_params=_cparams(("arbitrary",)),
        name="moe_combine",
    )(dest_blk, yb, h, g.reshape(1, d), b.reshape(1, d))


def _moe_layer(h, w_router, b_router, w_gu, b_gu, w_down, b_down, g, b, alpha):
    t, d = h.shape
    top_idx, top_w = _router(h, w_router, b_router)
    n_assign = t * TOP_K
    e_flat = top_idx.reshape(-1)
    onehot = (e_flat[:, None] == jnp.arange(N_EXPERTS, dtype=jnp.int32)[None, :]).astype(jnp.int32)
    rank = jnp.sum((jnp.cumsum(onehot, axis=0) - onehot) * onehot, axis=1)
    counts = jnp.sum(onehot, axis=0)
    padded = (counts + MOE_BLOCK - 1) // MOE_BLOCK * MOE_BLOCK
    pend = jnp.cumsum(padded)
    pstart = pend - padded
    dest = (pstart[e_flat] + rank).astype(jnp.int32)
    n_blocks = -(-n_assign // MOE_BLOCK) + N_EXPERTS
    n_slots = n_blocks * MOE_BLOCK
    tok_flat = jnp.repeat(jnp.arange(t, dtype=jnp.int32), TOP_K)
    slot_tok = jnp.zeros((n_slots,), jnp.int32).at[dest].set(tok_flat)
    slot_w = jnp.zeros((n_slots,), F32).at[dest].set(top_w.reshape(-1))
    block_e = jnp.minimum(
        jnp.searchsorted(pend, jnp.arange(n_blocks, dtype=jnp.int32) * MOE_BLOCK, side="right"),
        N_EXPERTS - 1).astype(jnp.int32)
    n_used = (pend[-1] // MOE_BLOCK).astype(jnp.int32).reshape(1)
    yb = _moe_experts(h, block_e, n_used, slot_tok, slot_w, w_gu, b_gu, w_down, b_down)
    tm = min(128, t)
    dest_blk = dest.reshape(t // tm, tm, TOP_K).transpose(0, 2, 1).reshape(-1)
    return _moe_combine(yb, dest_blk, h, g, b, alpha, tm)


def kernel(x, mem, positions, ln_in_g, ln_in_b, ln_mem_g, ln_mem_b, w_in, conv_w, conv_b, dt_bias, a_log, d_skip, ssd_norm_g, w_ret_o, w_ssd_o, w_mix_o, ln1_g, ln1_b, w_xq, w_xkv, w_xo, ln2_g, ln2_b, w_router, b_router, w_gu, b_gu, w_down, b_down, ln3_g, ln3_b):
    bsz, s, d = x.shape
    t = bsz * s
    depth = w_in.shape[0]
    alpha = (2.0 * depth) ** 0.25
    h, hb = _layernorm(x.reshape(t, d), ln_in_g, ln_in_b)
    _, memb = _layernorm(mem.reshape(bsz * N_MEM, d), ln_mem_g, ln_mem_b)
    cos, sin = _rope_tables(positions)
    for l in range(depth):
        w_main = jnp.concatenate([w_in[l][:, :OFF_DT], w_in[l][:, OFF_GATE_R:]], axis=1).astype(BF16)
        w_dt = w_in[l][:, OFF_DT:OFF_GATE_R].astype(BF16)
        proj = _matmul(hb, w_main, BF16, 1024, 1024, "in_proj")
        dtraw = _matmul(hb, w_dt, F32, 1024, 2 * SSD_HEADS, "dt_proj")
        proj3 = proj.reshape(bsz, s, MAIN_W)
        ret = _retention(proj3, cos, sin)
        xbc = _conv_silu(proj3, conv_w[l], conv_b[l])
        ssd = _ssd(xbc, proj3, dtraw.reshape(bsz, s, 2 * SSD_HEADS), dt_bias[l], a_log[l], d_skip[l],
                   ssd_norm_g[l])
        merged = _merge(ret.reshape(t, RET_V), ssd.reshape(t, SSD_D_INNER), proj,
                        w_ret_o[l].astype(BF16), w_ssd_o[l].astype(BF16))
        h, hb = _proj_res_ln(merged, w_mix_o[l].astype(BF16), h, ln1_g[l], ln1_b[l], alpha)

        q = _matmul(hb, w_xq[l].astype(BF16), BF16, 1024, 1024, "xattn_q")
        kv = _matmul(memb, w_xkv[l].astype(BF16), BF16, 512, 1024, "xattn_kv")
        h3, hb3 = _xattn(q.reshape(bsz, s, d), kv.reshape(bsz, N_MEM, 2 * d), w_xo[l].astype(BF16),
                         h.reshape(bsz, s, d), ln2_g[l], ln2_b[l], alpha)
        h = h3.reshape(t, d)

        h, hb = _moe_layer(h, w_router[l], b_router[l], w_gu[l].astype(BF16), b_gu[l],
                           w_down[l].astype(BF16), b_down[l], ln3_g[l], ln3_b[l], alpha)
    return h.reshape(bsz, s, d)
```

```python
import functools

import jax
import jax.numpy as jnp
from jax import lax
from jax.experimental import pallas as pl
from jax.experimental.pallas import tpu as pltpu

F32 = jnp.float32
BF16 = jnp.bfloat16

D_MODEL = 2048
N_MEM = 256
RET_HEADS = 8
RET_QK_DIM = 128
RET_V_DIM = 256
RET_Q = RET_HEADS * RET_QK_DIM
RET_V = RET_HEADS * RET_V_DIM
RET_CHUNK = 128
ROPE_BASE = 10000.0
SSD_D_INNER = 2 * D_MODEL
SSD_HEAD_DIM = 64
SSD_HEADS = SSD_D_INNER // SSD_HEAD_DIM
SSD_GROUPS = 8
SSD_HPG = SSD_HEADS // SSD_GROUPS
SSD_GROUP_W = SSD_D_INNER // SSD_GROUPS
SSD_STATE = 128
SSD_BC = SSD_GROUPS * SSD_STATE
SSD_CONV = 5
SSD_SUB = 64
CONV_CH = SSD_D_INNER + 2 * SSD_BC
XATTN_HEADS = 4
XATTN_HEAD_DIM = D_MODEL // XATTN_HEADS
N_EXPERTS = 32
TOP_K = 4
EXPERT_DIM = D_MODEL // 2
SWIGLU_LIMIT = 7.0
SWIGLU_ALPHA = 1.702
MOE_BLOCK = 256
LN_EPS = 1e-5
NEG_BIG = -1e30

OFF_Q = 0
OFF_K = OFF_Q + RET_Q
OFF_V = OFF_K + RET_Q
OFF_GRET = OFF_V + RET_V
OFF_Z = OFF_GRET + RET_V
OFF_XS = OFF_Z + SSD_D_INNER
OFF_BM = OFF_XS + SSD_D_INNER
OFF_CM = OFF_BM + SSD_BC
OFF_DT = OFF_CM + SSD_BC
OFF_GATE_R = OFF_DT + 2 * SSD_HEADS
OFF_GATE_S = OFF_GATE_R + D_MODEL
IN_WIDTH = OFF_GATE_S + D_MODEL
MAIN_W = IN_WIDTH - 2 * SSD_HEADS
M_GATE_R = OFF_DT
M_GATE_S = OFF_DT + D_MODEL

VMEM_LIMIT_MB = 48


def _cparams(sem, vmem_mb=VMEM_LIMIT_MB):
    return pltpu.CompilerParams(dimension_semantics=sem, vmem_limit_bytes=vmem_mb * 1024 * 1024)


def _dot(a, b):
    return jnp.dot(a, b, preferred_element_type=F32)


def _dot_nt(a, b):
    return lax.dot_general(a, b, (((1,), (1,)), ((), ())), preferred_element_type=F32)


def _dot_tn(a, b):
    return lax.dot_general(a, b, (((0,), (0,)), ((), ())), preferred_element_type=F32)


def _sigmoid(x):
    return 1.0 / (1.0 + jnp.exp(-x))


def _softplus(x):
    return jnp.maximum(x, 0.0) + jnp.log(1.0 + jnp.exp(-jnp.abs(x)))


def _split2(a):
    a1 = a.astype(BF16)
    return a1, (a - a1.astype(F32)).astype(BF16)


HALF_D = D_MODEL // 2
HI_MASK = 0xFFFF0000


def _pack_pairs(lo, hi):
    lo_bits = lax.bitcast_convert_type(lo.astype(BF16).astype(F32), jnp.uint32)
    hi_bits = lax.bitcast_convert_type(hi.astype(BF16).astype(F32), jnp.uint32)
    return (lo_bits >> 16) | (hi_bits & jnp.uint32(HI_MASK))


def _unpack_pairs(u):
    lo = lax.bitcast_convert_type(u << 16, F32)
    hi = lax.bitcast_convert_type(u & jnp.uint32(HI_MASK), F32)
    return lo, hi


def _ln_rows(x, g, b):
    mu = jnp.mean(x, axis=-1, keepdims=True)
    xc = x - mu
    var = jnp.mean(xc * xc, axis=-1, keepdims=True)
    return xc * lax.rsqrt(var + LN_EPS) * g + b


def _ln_kernel(x_ref, g_ref, b_ref, o_ref, ob_ref):
    y = _ln_rows(x_ref[...], g_ref[...], b_ref[...])
    o_ref[...] = y
    ob_ref[...] = y.astype(BF16)


def _layernorm(x, g, b):
    m, d = x.shape
    tm = min(512, m)
    return pl.pallas_call(
        _ln_kernel,
        grid=(m // tm,),
        in_specs=[pl.BlockSpec((tm, d), lambda i: (i, 0)),
                  pl.BlockSpec((1, d), lambda i: (0, 0)),
                  pl.BlockSpec((1, d), lambda i: (0, 0))],
        out_specs=[pl.BlockSpec((tm, d), lambda i: (i, 0)),
                   pl.BlockSpec((tm, d), lambda i: (i, 0))],
        out_shape=[jax.ShapeDtypeStruct((m, d), F32), jax.ShapeDtypeStruct((m, d), BF16)],
        compiler_params=_cparams(("parallel",)),
        name="layernorm",
    )(x, g.reshape(1, d), b.reshape(1, d))


def _mm_kernel(x_ref, w_ref, o_ref):
    o_ref[...] = _dot(x_ref[...], w_ref[...]).astype(o_ref.dtype)


def _matmul(x, w, out_dtype, tm, tn, name):
    m, k = x.shape
    n = w.shape[1]
    tm = min(tm, m)
    tn = min(tn, n)
    return pl.pallas_call(
        _mm_kernel,
        grid=(m // tm, n // tn),
        in_specs=[pl.BlockSpec((tm, k), lambda i, j: (i, 0)),
                  pl.BlockSpec((k, tn), lambda i, j: (0, j))],
        out_specs=pl.BlockSpec((tm, tn), lambda i, j: (i, j)),
        out_shape=jax.ShapeDtypeStruct((m, n), out_dtype),
        compiler_params=_cparams(("parallel", "arbitrary")),
        name=name,
    )(x, w)


def _rope_kernel(pos_ref, inv_ref, sgn_ref, cos_ref, sin_ref):
    ang = pos_ref[...] * inv_ref[...]
    cos_ref[...] = jnp.cos(ang)
    sin_ref[...] = jnp.sin(ang) * sgn_ref[...]


def _rope_tables(positions):
    t = positions.size
    half = RET_QK_DIM // 2
    inv = ROPE_BASE ** (-jnp.arange(half, dtype=F32) / half)
    inv2 = jnp.concatenate([inv, inv]).reshape(1, RET_QK_DIM)
    sgn = jnp.concatenate([-jnp.ones((half,), F32), jnp.ones((half,), F32)]).reshape(1, RET_QK_DIM)
    pos = positions.astype(F32).reshape(t, 1)
    tm = min(1024, t)
    return pl.pallas_call(
        _rope_kernel,
        grid=(t // tm,),
        in_specs=[pl.BlockSpec((tm, 1), lambda i: (i, 0)),
                  pl.BlockSpec((1, RET_QK_DIM), lambda i: (0, 0)),
                  pl.BlockSpec((1, RET_QK_DIM), lambda i: (0, 0))],
        out_specs=[pl.BlockSpec((tm, RET_QK_DIM), lambda i: (i, 0)),
                   pl.BlockSpec((tm, RET_QK_DIM), lambda i: (i, 0))],
        out_shape=[jax.ShapeDtypeStruct((t, RET_QK_DIM), F32)] * 2,
        compiler_params=_cparams(("parallel",)),
        name="rope_tables",
    )(pos, inv2, sgn)


def _ret_kernel(*refs, reverse, nchunk):
    if not reverse:
        (q_ref, k_ref, v_ref, cos_ref, sin_ref, qd_ref, kd_ref, cd_ref, dm_ref, y_ref, st_ref) = refs
    else:
        (q_ref, k_ref, v_ref, cos_ref, sin_ref, qd_ref, kd_ref, cd_ref, yf_ref, g_ref, o_ref, st_ref) = refs

    @pl.when(pl.program_id(2) == 0)
    def _():
        st_ref[...] = jnp.zeros_like(st_ref)

    qd = qd_ref[...]
    kd = kd_ref[...]
    cd = cd_ref[...]
    scale = RET_QK_DIM ** -0.5
    c = RET_CHUNK
    order = range(nchunk - 1, -1, -1) if reverse else range(nchunk)
    for ci in order:
        sl = slice(ci * c, (ci + 1) * c)
        cs = cos_ref[sl, :]
        sn = sin_ref[sl, :]
        q = q_ref[sl, :].astype(F32)
        k = k_ref[sl, :].astype(F32)
        v = v_ref[sl, :]
        qr = q * cs + pltpu.roll(q, RET_QK_DIM // 2, 1) * sn
        kr = (k * cs + pltpu.roll(k, RET_QK_DIM // 2, 1) * sn) * scale
        st = st_ref[...]
        y = _dot((qr * qd).astype(BF16), st.astype(BF16))
        st_ref[...] = cd * st + _dot_tn((kr * kd).astype(BF16), v)
        if not reverse:
            s = _dot_nt(qr.astype(BF16), kr.astype(BF16)) * dm_ref[...]
            y_ref[sl, :] = y + _dot(s.astype(BF16), v)
        else:
            tot = yf_ref[sl, :] + y
            mu = jnp.mean(tot, axis=-1, keepdims=True)
            tc = tot - mu
            var = jnp.mean(tc * tc, axis=-1, keepdims=True)
            g = g_ref[sl, :].astype(F32)
            o_ref[sl, :] = (tc * lax.rsqrt(var + LN_EPS) * (g * _sigmoid(g))).astype(BF16)


def _ret_decay_tables():
    lg = jnp.log1p(-jnp.exp2(-5.0 - jnp.arange(RET_HEADS, dtype=F32)))
    idx = jnp.arange(RET_CHUNK, dtype=F32)
    c = float(RET_CHUNK)
    dist = jnp.abs(idx[:, None] - idx[None, :])
    dmat = jnp.exp(lg[:, None, None] * dist)

    def rows(e):
        return jnp.broadcast_to(jnp.exp(lg[:, None] * e)[..., None], (RET_HEADS, RET_CHUNK, RET_QK_DIM))

    qd_f = rows(idx + 1.0)
    kd_f = rows(c - 1.0 - idx)
    qd_b = rows(c - idx)
    kd_b = rows(idx)
    cd = jnp.broadcast_to(jnp.exp(lg * c)[:, None, None], (RET_HEADS, 1, RET_V_DIM))
    return dmat, (qd_f, kd_f), (qd_b, kd_b), cd


def _retention(proj3, cos, sin):
    b, s, _ = proj3.shape
    rb = min(512, s)
    nb = s // rb
    nchunk = rb // RET_CHUNK
    dmat, dec_f, dec_b, cd = _ret_decay_tables()
    cos3 = cos.reshape(b, s, RET_QK_DIM)
    sin3 = sin.reshape(b, s, RET_QK_DIM)
    kq = OFF_K // RET_QK_DIM
    kv = OFF_V // RET_V_DIM
    kg = OFF_GRET // RET_V_DIM
    tab = lambda w: pl.BlockSpec((None, RET_CHUNK, w), lambda bi, h, i: (h, 0, 0))
    cd_spec = pl.BlockSpec((None, 1, RET_V_DIM), lambda bi, h, i: (h, 0, 0))

    def specs(rev):
        blk = (lambda i: nb - 1 - i) if rev else (lambda i: i)
        return [
            pl.BlockSpec((None, rb, RET_QK_DIM), lambda bi, h, i: (bi, blk(i), h)),
            pl.BlockSpec((None, rb, RET_QK_DIM), lambda bi, h, i: (bi, blk(i), kq + h)),
            pl.BlockSpec((None, rb, RET_V_DIM), lambda bi, h, i: (bi, blk(i), kv + h)),
            pl.BlockSpec((None, rb, RET_QK_DIM), lambda bi, h, i: (bi, blk(i), 0)),
            pl.BlockSpec((None, rb, RET_QK_DIM), lambda bi, h, i: (bi, blk(i), 0)),
            tab(RET_QK_DIM), tab(RET_QK_DIM), cd_spec,
        ], blk

    sem = ("parallel", "parallel", "arbitrary")
    scratch = [pltpu.VMEM((RET_QK_DIM, RET_V_DIM), F32)]
    in_f, _ = specs(False)
    y_f = pl.pallas_call(
        functools.partial(_ret_kernel, reverse=False, nchunk=nchunk),
        grid=(b, RET_HEADS, nb),
        in_specs=in_f + [tab(RET_CHUNK)],
        out_specs=pl.BlockSpec((None, rb, RET_V_DIM), lambda bi, h, i: (bi, i, h)),
        out_shape=jax.ShapeDtypeStruct((b, s, RET_V), F32),
        scratch_shapes=scratch,
        compiler_params=_cparams(sem),
        name="retention_fwd",
    )(proj3, proj3, proj3, cos3, sin3, dec_f[0], dec_f[1], cd, dmat)
    in_b, blk = specs(True)
    out = pl.pallas_call(
        functools.partial(_ret_kernel, reverse=True, nchunk=nchunk),
        grid=(b, RET_HEADS, nb),
        in_specs=in_b + [
            pl.BlockSpec((None, rb, RET_V_DIM), lambda bi, h, i: (bi, blk(i), h)),
            pl.BlockSpec((None, rb, RET_V_DIM), lambda bi, h, i: (bi, blk(i), kg + h)),
        ],
        out_specs=pl.BlockSpec((None, rb, RET_V_DIM), lambda bi, h, i: (bi, blk(i), h)),
        out_shape=jax.ShapeDtypeStruct((b, s, RET_V), BF16),
        scratch_shapes=scratch,
        compiler_params=_cparams(sem),
        name="retention_bwd",
    )(proj3, proj3, proj3, cos3, sin3, dec_b[0], dec_b[1], cd, y_f, proj3)
    return out


CONV_HALO = 16


def _conv_kernel(xm_ref, xp_ref, xn_ref, w_ref, b_ref, o_ref, ext_ref):
    i = pl.program_id(1)
    n = pl.num_programs(1)
    tm = xm_ref.shape[0]
    pad = SSD_CONV // 2
    ext_ref[0:CONV_HALO, :] = jnp.where(i == 0, 0.0, xp_ref[...].astype(F32))
    ext_ref[CONV_HALO:CONV_HALO + tm, :] = xm_ref[...].astype(F32)
    ext_ref[CONV_HALO + tm:2 * CONV_HALO + tm, :] = jnp.where(i == n - 1, 0.0, xn_ref[...].astype(F32))
    acc = b_ref[...] + w_ref[0:1, :] * ext_ref[CONV_HALO - pad:CONV_HALO - pad + tm, :]
    for k in range(1, SSD_CONV):
        o = CONV_HALO - pad + k
        acc = acc + w_ref[k:k + 1, :] * ext_ref[o:o + tm, :]
    o_ref[...] = (acc * _sigmoid(acc)).astype(BF16)


def _conv_silu(proj3, conv_w, conv_b):
    b, s, _ = proj3.shape
    tm = min(512, s)
    tc = 512
    ns = s // tm
    c0 = OFF_XS // tc
    hb = tm // CONV_HALO
    last = s // CONV_HALO - 1
    return pl.pallas_call(
        _conv_kernel,
        grid=(b, ns, CONV_CH // tc),
        in_specs=[
            pl.BlockSpec((None, tm, tc), lambda bi, i, j: (bi, i, c0 + j)),
            pl.BlockSpec((None, CONV_HALO, tc), lambda bi, i, j: (bi, jnp.maximum(i * hb - 1, 0), c0 + j)),
            pl.BlockSpec((None, CONV_HALO, tc), lambda bi, i, j: (bi, jnp.minimum((i + 1) * hb, last), c0 + j)),
            pl.BlockSpec((SSD_CONV, tc), lambda bi, i, j: (0, j)),
            pl.BlockSpec((1, tc), lambda bi, i, j: (0, j)),
        ],
        out_specs=pl.BlockSpec((None, tm, tc), lambda bi, i, j: (bi, i, j)),
        out_shape=jax.ShapeDtypeStruct((b, s, CONV_CH), BF16),
        scratch_shapes=[pltpu.VMEM((tm + 2 * CONV_HALO, tc), F32)],
        compiler_params=_cparams(("parallel", "parallel", "parallel")),
        name="conv_silu",
    )(proj3, proj3, proj3, conv_w, conv_b.reshape(1, CONV_CH))


def _ssd_kernel(*refs, reverse, nchunk):
    if not reverse:
        (x_ref, b_ref, c_ref, dt_ref, bias_ref, alog_ref, y_ref, h_ref) = refs
    else:
        (x_ref, b_ref, c_ref, dt_ref, bias_ref, alog_ref,
         yf_ref, z_ref, dsk_ref, ng_ref, o_ref, h_ref) = refs

    @pl.when(pl.program_id(2) == 0)
    def _():
        h_ref[...] = jnp.zeros_like(h_ref)

    g = pl.program_id(1)
    sub = SSD_SUB
    ch = 2 * sub
    gw = SSD_GROUP_W
    base = (SSD_HEADS if reverse else 0) + g * SSD_HPG
    i32 = jnp.int32
    ci = lax.broadcasted_iota(i32, (ch, gw), 0)
    li = lax.broadcasted_iota(i32, (ch, gw), 1)
    selw = (ci == base + (li >> 6)).astype(BF16)
    rn = lax.broadcasted_iota(i32, (16, ch), 0)
    cn = lax.broadcasted_iota(i32, (16, ch), 1)
    seln = jnp.logical_and(cn == base + rn, rn < SSD_HPG).astype(BF16)
    ii = lax.broadcasted_iota(i32, (ch, ch), 0)
    jj = lax.broadcasted_iota(i32, (ch, ch), 1)
    tri = ((jj >= ii) if reverse else (jj <= ii)).astype(BF16)
    jh = jj & (sub - 1)
    bmask = (ii >> 6) == (jj >> 6)
    if reverse:
        mask_full = (jh + sub) > ii
        mask_half = (jh > ii)[:sub]
    else:
        mask_full = ii >= jh
        mask_half = (ii >= jh)[:sub]
    bias = bias_ref[...]
    a_neg = -jnp.exp(alog_ref[...])

    hst = h_ref[...]
    for t in range(nchunk):
        sc = (nchunk - 1 - t) if reverse else t
        rows = slice(sc * ch, (sc + 1) * ch)
        dt_all = _softplus(dt_ref[rows, :] + bias)
        p_all = sum(_dot(tri, s) for s in _split2(dt_all * a_neg))
        ps = _split2(p_all)
        dtw = sum(_dot(s, selw) for s in _split2(dt_all))
        pw = sum(_dot(s, selw) for s in ps)
        pn = sum(_dot_nt(seln, s) for s in ps)
        x = x_ref[rows, :].astype(F32)
        xdt = x * dtw
        bm = b_ref[rows, :]
        cm = c_ref[rows, :]
        bm_a = jnp.concatenate([bm[:sub], bm[:sub]], axis=0)
        bm_b = jnp.concatenate([bm[sub:], bm[sub:]], axis=0)
        if reverse:
            cb_full = _dot_nt(cm, bm_b)
            cb_half = _dot_nt(cm[:sub], bm_a)
        else:
            cb_full = _dot_nt(cm, bm_a)
            cb_half = _dot_nt(cm[sub:], bm_b)
        parts = []
        for p in range(SSD_HPG // 2):
            colp = pw[:, p * ch:(p + 1) * ch]
            row_a = jnp.concatenate([pn[2 * p:2 * p + 1, :sub], pn[2 * p + 1:2 * p + 2, :sub]], axis=1)
            row_b = jnp.concatenate([pn[2 * p:2 * p + 1, sub:], pn[2 * p + 1:2 * p + 2, sub:]], axis=1)
            xp = xdt[:, p * ch:(p + 1) * ch].astype(BF16)
            xb_a = jnp.where(bmask, jnp.concatenate([xp[:sub], xp[:sub]], axis=0), 0)
            xb_b = jnp.where(bmask, jnp.concatenate([xp[sub:], xp[sub:]], axis=0), 0)
            if reverse:
                w_full = (cb_full * jnp.exp(jnp.where(mask_full, colp - row_b, NEG_BIG))).astype(BF16)
                w_half = (cb_half * jnp.exp(jnp.where(mask_half, colp[:sub] - row_a, NEG_BIG))).astype(BF16)
                y_full = _dot(w_full, xb_b)
                y_half = _dot(w_half, xb_a)
                parts.append(jnp.concatenate([y_full[:sub] + y_half, y_full[sub:]], axis=0))
            else:
                w_full = (cb_full * jnp.exp(jnp.where(mask_full, colp - row_a, NEG_BIG))).astype(BF16)
                w_half = (cb_half * jnp.exp(jnp.where(mask_half, colp[sub:] - row_b, NEG_BIG))).astype(BF16)
                y_full = _dot(w_full, xb_a)
                y_half = _dot(w_half, xb_b)
                parts.append(jnp.concatenate([y_full[:sub], y_full[sub:] + y_half], axis=0))
        y = jnp.concatenate(parts, axis=1) + _dot(cm, hst.astype(BF16)) * jnp.exp(pw)
        plast = pw[0:1, :] if reverse else pw[ch - 1:ch, :]
        xdec = (jnp.exp(plast - pw) * xdt).astype(BF16)
        hst = jnp.exp(plast) * hst + _dot_tn(bm, xdec)
        if not reverse:
            y_ref[rows, :] = y
        else:
            tot = yf_ref[rows, :] + y + dsk_ref[...] * x
            z = z_ref[rows, :].astype(F32)
            tot = tot * (z * _sigmoid(z))
            ms = jnp.mean(tot * tot, axis=-1, keepdims=True)
            o_ref[rows, :] = (tot * lax.rsqrt(ms + LN_EPS) * ng_ref[...]).astype(BF16)
    h_ref[...] = hst


def _ssd(xbc, proj3, dtraw3, dt_bias, a_log, d_skip, norm_g):
    b, s, _ = xbc.shape
    rb = min(512, s)
    nb = s // rb
    nchunk = rb // (2 * SSD_SUB)
    gw = SSD_GROUP_W
    bias = dt_bias.astype(F32).reshape(1, 2 * SSD_HEADS)
    alog = a_log.astype(F32).reshape(1, 2 * SSD_HEADS)
    dsk = jnp.broadcast_to(d_skip.astype(F32)[:, None], (SSD_HEADS, SSD_HEAD_DIM)).reshape(SSD_GROUPS, 1, gw)
    ng = norm_g.astype(F32).reshape(SSD_GROUPS, 1, gw)
    kb = SSD_D_INNER // SSD_STATE
    kc = (SSD_D_INNER + SSD_BC) // SSD_STATE
    kz = OFF_Z // gw

    def specs(rev):
        blk = (lambda i: nb - 1 - i) if rev else (lambda i: i)
        small = pl.BlockSpec((1, 2 * SSD_HEADS), lambda bi, g, i: (0, 0))
        return [
            pl.BlockSpec((None, rb, gw), lambda bi, g, i: (bi, blk(i), g)),
            pl.BlockSpec((None, rb, SSD_STATE), lambda bi, g, i: (bi, blk(i), kb + g)),
            pl.BlockSpec((None, rb, SSD_STATE), lambda bi, g, i: (bi, blk(i), kc + g)),
            pl.BlockSpec((None, rb, 2 * SSD_HEADS), lambda bi, g, i: (bi, blk(i), 0)),
            small, small,
        ], blk

    sem = ("parallel", "parallel", "arbitrary")
    scratch = [pltpu.VMEM((SSD_STATE, gw), F32)]
    in_f, _ = specs(False)
    y_f = pl.pallas_call(
        functools.partial(_ssd_kernel, reverse=False, nchunk=nchunk),
        grid=(b, SSD_GROUPS, nb),
        in_specs=in_f,
        out_specs=pl.BlockSpec((None, rb, gw), lambda bi, g, i: (bi, i, g)),
        out_shape=jax.ShapeDtypeStruct((b, s, SSD_D_INNER), F32),
        scratch_shapes=scratch,
        compiler_params=_cparams(sem),
        name="ssd_fwd",
    )(xbc, xbc, xbc, dtraw3, bias, alog)
    in_b, blk = specs(True)
    grp = pl.BlockSpec((None, 1, gw), lambda bi, g, i: (g, 0, 0))
    out = pl.pallas_call(
        functools.partial(_ssd_kernel, reverse=True, nchunk=nchunk),
        grid=(b, SSD_GROUPS, nb),
        in_specs=in_b + [
            pl.BlockSpec((None, rb, gw), lambda bi, g, i: (bi, blk(i), g)),
            pl.BlockSpec((None, rb, gw), lambda bi, g, i: (bi, blk(i), kz + g)),
            grp, grp,
        ],
        out_specs=pl.BlockSpec((None, rb, gw), lambda bi, g, i: (bi, blk(i), g)),
        out_shape=jax.ShapeDtypeStruct((b, s, SSD_D_INNER), BF16),
        scratch_shapes=scratch,
        compiler_params=_cparams(sem),
        name="ssd_bwd",
    )(xbc, xbc, xbc, dtraw3, bias, alog, y_f, proj3, dsk, ng)
    return out


def _merge_kernel(ret_ref, ssd_ref, wr_ref, ws_ref, gr_ref, gs_ref, o_ref):
    yr = _dot(ret_ref[...], wr_ref[...])
    ys = _dot(ssd_ref[...], ws_ref[...])
    o = _sigmoid(gr_ref[...].astype(F32)) * yr + _sigmoid(gs_ref[...].astype(F32)) * ys
    o_ref[...] = o.astype(BF16)


def _merge(ret, ssd, proj, w_ret_o, w_ssd_o):
    m = ret.shape[0]
    tm = min(1024, m)
    tn = 256
    gr0 = M_GATE_R // tn
    gs0 = M_GATE_S // tn
    return pl.pallas_call(
        _merge_kernel,
        grid=(m // tm, D_MODEL // tn),
        in_specs=[
            pl.BlockSpec((tm, RET_V), lambda i, j: (i, 0)),
            pl.BlockSpec((tm, SSD_D_INNER), lambda i, j: (i, 0)),
            pl.BlockSpec((RET_V, tn), lambda i, j: (0, j)),
            pl.BlockSpec((SSD_D_INNER, tn), lambda i, j: (0, j)),
            pl.BlockSpec((tm, tn), lambda i, j: (i, gr0 + j)),
            pl.BlockSpec((tm, tn), lambda i, j: (i, gs0 + j)),
        ],
        out_specs=pl.BlockSpec((tm, tn), lambda i, j: (i, j)),
        out_shape=jax.ShapeDtypeStruct((m, D_MODEL), BF16),
        compiler_params=_cparams(("parallel", "arbitrary")),
        name="branch_merge",
    )(ret, ssd, w_ret_o, w_ssd_o, proj, proj)


def _proj_ln_kernel(x_ref, w_ref, h_ref, g_ref, b_ref, o_ref, ob_ref, *, alpha):
    y = _dot(x_ref[...], w_ref[...]) + alpha * h_ref[...]
    y = _ln_rows(y, g_ref[...], b_ref[...])
    o_ref[...] = y
    ob_ref[...] = y.astype(BF16)


def _proj_res_ln(x, w, h, g, b, alpha):
    m, k = x.shape
    d = w.shape[1]
    tm = min(256, m)
    row = lambda i: (i, 0)
    fix = lambda i: (0, 0)
    return pl.pallas_call(
        functools.partial(_proj_ln_kernel, alpha=alpha),
        grid=(m // tm,),
        in_specs=[pl.BlockSpec((tm, k), row), pl.BlockSpec((k, d), fix), pl.BlockSpec((tm, d), row),
                  pl.BlockSpec((1, d), fix), pl.BlockSpec((1, d), fix)],
        out_specs=[pl.BlockSpec((tm, d), row), pl.BlockSpec((tm, d), row)],
        out_shape=[jax.ShapeDtypeStruct((m, d), F32), jax.ShapeDtypeStruct((m, d), BF16)],
        compiler_params=_cparams(("parallel",)),
        name="proj_residual_ln",
    )(x, w, h, g.reshape(1, d), b.reshape(1, d))


def _xattn_kernel(q_ref, k_ref, v_ref, wo_ref, h_ref, g_ref, b_ref, o_ref, op_ref, *, alpha):
    scale = XATTN_HEAD_DIM ** -0.5
    outs = []
    for hd in range(XATTN_HEADS):
        sl = slice(hd * XATTN_HEAD_DIM, (hd + 1) * XATTN_HEAD_DIM)
        s = _dot_nt(q_ref[:, sl], k_ref[:, sl]) * scale
        e = jnp.exp(s - jnp.max(s, axis=-1, keepdims=True))
        p = e / jnp.sum(e, axis=-1, keepdims=True)
        outs.append(_dot(p.astype(BF16), v_ref[:, sl]).astype(BF16))
    o = jnp.concatenate(outs, axis=1)
    y = _dot(o, wo_ref[...]) + alpha * h_ref[...]
    y = _ln_rows(y, g_ref[...], b_ref[...])
    o_ref[...] = y
    op_ref[...] = _pack_pairs(y[:, :HALF_D], y[:, HALF_D:])


def _xattn(q3, kv3, w_xo, h3, g, b, alpha):
    bsz, s, d = q3.shape
    tm = min(256, s)
    row = lambda bi, i: (bi, i, 0)
    fix = lambda bi, i: (0, 0)
    return pl.pallas_call(
        functools.partial(_xattn_kernel, alpha=alpha),
        grid=(bsz, s // tm),
        in_specs=[
            pl.BlockSpec((None, tm, d), row),
            pl.BlockSpec((None, N_MEM, d), lambda bi, i: (bi, 0, 0)),
            pl.BlockSpec((None, N_MEM, d), lambda bi, i: (bi, 0, 1)),
            pl.BlockSpec((d, d), fix),
            pl.BlockSpec((None, tm, d), row),
            pl.BlockSpec((1, d), fix), pl.BlockSpec((1, d), fix),
        ],
        out_specs=[pl.BlockSpec((None, tm, d), row), pl.BlockSpec((None, tm, HALF_D), row)],
        out_shape=[jax.ShapeDtypeStruct((bsz, s, d), F32), jax.ShapeDtypeStruct((bsz, s, HALF_D), jnp.uint32)],
        compiler_params=_cparams(("parallel", "parallel")),
        name="memory_xattn",
    )(q3, kv3, kv3, w_xo, h3, g.reshape(1, d), b.reshape(1, d))


ROUTE_LANES = 128


def _router_kernel(h_ref, w_ref, b_ref, idx_ref, wt_ref):
    lg = jnp.dot(h_ref[...], w_ref[...], preferred_element_type=F32,
                 precision=lax.Precision.HIGHEST) + b_ref[...]
    tm = lg.shape[0]
    lane = lax.broadcasted_iota(jnp.int32, (tm, N_EXPERTS), 1).astype(F32)
    vals, idxs = [], []
    for _ in range(TOP_K):
        m = jnp.max(lg, axis=-1, keepdims=True)
        am = jnp.min(jnp.where(lg == m, lane, float(N_EXPERTS)), axis=-1, keepdims=True)
        vals.append(m)
        idxs.append(am.astype(jnp.int32))
        lg = jnp.where(lane == am, -jnp.inf, lg)
    es = [jnp.exp(v - vals[0]) for v in vals]
    tot = es[0] + es[1] + es[2] + es[3]
    out_lane = lax.broadcasted_iota(jnp.int32, (tm, ROUTE_LANES), 1)
    io = jnp.zeros((tm, ROUTE_LANES), jnp.int32)
    wo = jnp.zeros((tm, ROUTE_LANES), F32)
    for k in range(TOP_K):
        io = jnp.where(out_lane == k, idxs[k], io)
        wo = jnp.where(out_lane == k, es[k] / tot, wo)
    idx_ref[...] = io
    wt_ref[...] = wo


def _router(h, w_router, b_router):
    m, d = h.shape
    tm = min(512, m)
    row = lambda i: (i, 0)
    fix = lambda i: (0, 0)
    idx, wt = pl.pallas_call(
        _router_kernel,
        grid=(m // tm,),
        in_specs=[pl.BlockSpec((tm, d), row), pl.BlockSpec((d, N_EXPERTS), fix),
                  pl.BlockSpec((1, N_EXPERTS), fix)],
        out_specs=[pl.BlockSpec((tm, ROUTE_LANES), row), pl.BlockSpec((tm, ROUTE_LANES), row)],
        out_shape=[jax.ShapeDtypeStruct((m, ROUTE_LANES), jnp.int32),
                   jax.ShapeDtypeStruct((m, ROUTE_LANES), F32)],
        compiler_params=_cparams(("parallel",)),
        name="router_topk",
    )(h, w_router, b_router.reshape(1, N_EXPERTS))
    return idx[:, :TOP_K], wt


GATHER_UNROLL = 4


def _gather_rows(idx_ref, base, r0, n, src_hbm, dst, sem, priorities):
    np_ = len(priorities)

    def body(p, c):
        for u in range(np_):
            r = r0 + np_ * p + u
            row = idx_ref[base + r]
            pltpu.make_async_copy(src_hbm.at[pl.ds(row, 1), :], dst.at[pl.ds(r, 1), :], sem).start(
                priority=priorities[u])
        return c
    lax.fori_loop(0, n // np_, body, 0, unroll=GATHER_UNROLL)


def _wait_rows(n, src_hbm, dst, sem):
    pltpu.make_async_copy(src_hbm.at[pl.ds(0, n), :], dst, sem).wait()


def _moe_kernel(be_ref, nu_ref, tok_ref, hp_hbm, wgu_ref, bgu_ref, wd_ref, bd_ref, o_ref, xbuf, sem):
    i = pl.program_id(0)
    nused = nu_ref[0]
    mb = MOE_BLOCK
    prio = (1, 1)

    @pl.when(jnp.logical_and(i == 0, nused > 0))
    def _():
        _gather_rows(tok_ref, 0, 0, mb, hp_hbm, xbuf.at[0], sem.at[0], prio)

    @pl.when(i + 1 < nused)
    def _():
        nxt = (i + 1) % 2
        _gather_rows(tok_ref, (i + 1) * mb, 0, mb, hp_hbm, xbuf.at[nxt], sem.at[nxt], prio)

    @pl.when(i < nused)
    def _():
        slot = i % 2
        _wait_rows(mb, hp_hbm, xbuf.at[slot], sem.at[slot])
        lo, hi = _unpack_pairs(xbuf[slot])
        x = jnp.concatenate([lo.astype(BF16), hi.astype(BF16)], axis=1)
        gu = _dot(x, wgu_ref[...]) + bgu_ref[...]
        gate = jnp.minimum(gu[:, :EXPERT_DIM], SWIGLU_LIMIT)
        up = jnp.clip(gu[:, EXPERT_DIM:], -SWIGLU_LIMIT, SWIGLU_LIMIT)
        act = (up + 1.0) * gate * _sigmoid(gate * SWIGLU_ALPHA)
        ye = _dot(act.astype(BF16), wd_ref[...]) + bd_ref[...]
        o_ref[...] = _pack_pairs(ye[:, :HALF_D], ye[:, HALF_D:])

    @pl.when(i >= nused)
    def _():
        o_ref[...] = jnp.zeros_like(o_ref)


def _moe_experts(hp, block_e, n_used, slot_tok, w_gu, b_gu, w_down, b_down):
    d = D_MODEL
    nblk = block_e.shape[0]
    mb = MOE_BLOCK
    grid_spec = pltpu.PrefetchScalarGridSpec(
        num_scalar_prefetch=3,
        grid=(nblk,),
        in_specs=[
            pl.BlockSpec(memory_space=pl.ANY),
            pl.BlockSpec((None, d, 2 * EXPERT_DIM), lambda i, be, nu, tk: (be[i], 0, 0)),
            pl.BlockSpec((None, 1, 2 * EXPERT_DIM), lambda i, be, nu, tk: (be[i], 0, 0)),
            pl.BlockSpec((None, EXPERT_DIM, d), lambda i, be, nu, tk: (be[i], 0, 0)),
            pl.BlockSpec((None, 1, d), lambda i, be, nu, tk: (be[i], 0, 0)),
        ],
        out_specs=pl.BlockSpec((mb, HALF_D), lambda i, be, nu, tk: (i, 0)),
        scratch_shapes=[pltpu.VMEM((2, mb, HALF_D), jnp.uint32), pltpu.SemaphoreType.DMA((2,))],
    )
    return pl.pallas_call(
        _moe_kernel,
        grid_spec=grid_spec,
        out_shape=jax.ShapeDtypeStruct((nblk * mb, HALF_D), jnp.uint32),
        compiler_params=_cparams(("arbitrary",)),
        name="moe_experts",
    )(block_e, n_used, slot_tok, hp, w_gu, b_gu.reshape(N_EXPERTS, 1, -1), w_down,
      b_down.reshape(N_EXPERTS, 1, -1))


def _combine_kernel(dest_ref, yb_hbm, wt_ref, h_ref, g_ref, b_ref, o_ref, ob_ref, gbuf, sem, *, alpha, tm):
    i = pl.program_id(0)
    n = pl.num_programs(0)
    rows = TOP_K * tm
    prio = (0, 1)

    @pl.when(i == 0)
    def _():
        _gather_rows(dest_ref, 0, 0, rows, yb_hbm, gbuf.at[0], sem.at[0], prio)

    @pl.when(i + 1 < n)
    def _():
        nxt = (i + 1) % 2
        _gather_rows(dest_ref, (i + 1) * rows, 0, rows, yb_hbm, gbuf.at[nxt], sem.at[nxt], prio)

    slot = i % 2
    _wait_rows(rows, yb_hbm, gbuf.at[slot], sem.at[slot])
    wt = wt_ref[...]
    flo = fhi = None
    for k in range(TOP_K):
        lo, hi = _unpack_pairs(gbuf[slot, k * tm:(k + 1) * tm, :])
        wk = wt[:, k:k + 1]
        flo = wk * lo if flo is None else flo + wk * lo
        fhi = wk * hi if fhi is None else fhi + wk * hi
    ff = jnp.concatenate([flo, fhi], axis=1)
    y = _ln_rows(alpha * h_ref[...] + ff, g_ref[...], b_ref[...])
    o_ref[...] = y
    ob_ref[...] = y.astype(BF16)


def _moe_combine(yb, dest_blk, top_w, h, g, b, alpha, tm):
    t, d = h.shape
    row = lambda i, ds: (i, 0)
    fix = lambda i, ds: (0, 0)
    grid_spec = pltpu.PrefetchScalarGridSpec(
        num_scalar_prefetch=1,
        grid=(t // tm,),
        in_specs=[pl.BlockSpec(memory_space=pl.ANY), pl.BlockSpec((tm, ROUTE_LANES), row),
                  pl.BlockSpec((tm, d), row), pl.BlockSpec((1, d), fix), pl.BlockSpec((1, d), fix)],
        out_specs=[pl.BlockSpec((tm, d), row), pl.BlockSpec((tm, d), row)],
        scratch_shapes=[pltpu.VMEM((2, TOP_K * tm, HALF_D), jnp.uint32), pltpu.SemaphoreType.DMA((2,))],
    )
    return pl.pallas_call(
        functools.partial(_combine_kernel, alpha=alpha, tm=tm),
        grid_spec=grid_spec,
        out_shape=[jax.ShapeDtypeStruct((t, d), F32), jax.ShapeDtypeStruct((t, d), BF16)],
        compiler_params=_cparams(("arbitrary",)),
        name="moe_combine",
    )(dest_blk, yb, top_w, h, g.reshape(1, d), b.reshape(1, d))


def _moe_layer(h, hp, w_router, b_router, w_gu, b_gu, w_down, b_down, g, b, alpha):
    t, d = h.shape
    top_idx, top_w = _router(h, w_router, b_router)
    n_assign = t * TOP_K
    e_flat = top_idx.reshape(-1)
    onehot = (e_flat[:, None] == jnp.arange(N_EXPERTS, dtype=jnp.int32)[None, :]).astype(jnp.int32)
    rank = jnp.sum((jnp.cumsum(onehot, axis=0) - onehot) * onehot, axis=1)
    counts = jnp.sum(onehot, axis=0)
    padded = (counts + MOE_BLOCK - 1) // MOE_BLOCK * MOE_BLOCK
    pend = jnp.cumsum(padded)
    pstart = pend - padded
    dest = (pstart[e_flat] + rank).astype(jnp.int32)
    n_blocks = -(-n_assign // MOE_BLOCK) + N_EXPERTS
    n_slots = n_blocks * MOE_BLOCK
    tok_flat = jnp.repeat(jnp.arange(t, dtype=jnp.int32), TOP_K)
    slot_tok = jnp.zeros((n_slots,), jnp.int32).at[dest].set(tok_flat)
    block_e = jnp.minimum(
        jnp.searchsorted(pend, jnp.arange(n_blocks, dtype=jnp.int32) * MOE_BLOCK, side="right"),
        N_EXPERTS - 1).astype(jnp.int32)
    n_used = (pend[-1] // MOE_BLOCK).astype(jnp.int32).reshape(1)
    yb = _moe_experts(hp, block_e, n_used, slot_tok, w_gu, b_gu, w_down, b_down)
    tm = min(128, t)
    dest_blk = dest.reshape(t // tm, tm, TOP_K).transpose(0, 2, 1).reshape(-1)
    return _moe_combine(yb, dest_blk, top_w, h, g, b, alpha, tm)


def kernel(x, mem, positions, ln_in_g, ln_in_b, ln_mem_g, ln_mem_b, w_in, conv_w, conv_b, dt_bias, a_log, d_skip, ssd_norm_g, w_ret_o, w_ssd_o, w_mix_o, ln1_g, ln1_b, w_xq, w_xkv, w_xo, ln2_g, ln2_b, w_router, b_router, w_gu, b_gu, w_down, b_down, ln3_g, ln3_b):
    bsz, s, d = x.shape
    t = bsz * s
    depth = w_in.shape[0]
    alpha = (2.0 * depth) ** 0.25
    h, hb = _layernorm(x.reshape(t, d), ln_in_g, ln_in_b)
    _, memb = _layernorm(mem.reshape(bsz * N_MEM, d), ln_mem_g, ln_mem_b)
    cos, sin = _rope_tables(positions)
    for l in range(depth):
        w_main = jnp.concatenate([w_in[l][:, :OFF_DT], w_in[l][:, OFF_GATE_R:]], axis=1).astype(BF16)
        w_dt = w_in[l][:, OFF_DT:OFF_GATE_R].astype(BF16)
        proj = _matmul(hb, w_main, BF16, 1024, 1024, "in_proj")
        dtraw = _matmul(hb, w_dt, F32, 1024, 2 * SSD_HEADS, "dt_proj")
        proj3 = proj.reshape(bsz, s, MAIN_W)
        ret = _retention(proj3, cos, sin)
        xbc = _conv_silu(proj3, conv_w[l], conv_b[l])
        ssd = _ssd(xbc, proj3, dtraw.reshape(bsz, s, 2 * SSD_HEADS), dt_bias[l], a_log[l], d_skip[l],
                   ssd_norm_g[l])
        merged = _merge(ret.reshape(t, RET_V), ssd.reshape(t, SSD_D_INNER), proj,
                        w_ret_o[l].astype(BF16), w_ssd_o[l].astype(BF16))
        h, hb = _proj_res_ln(merged, w_mix_o[l].astype(BF16), h, ln1_g[l], ln1_b[l], alpha)

        q = _matmul(hb, w_xq[l].astype(BF16), BF16, 1024, 1024, "xattn_q")
        kv = _matmul(memb, w_xkv[l].astype(BF16), BF16, 512, 1024, "xattn_kv")
        h3, hp3 = _xattn(q.reshape(bsz, s, d), kv.reshape(bsz, N_MEM, 2 * d), w_xo[l].astype(BF16),
                         h.reshape(bsz, s, d), ln2_g[l], ln2_b[l], alpha)
        h = h3.reshape(t, d)

        h, hb = _moe_layer(h, hp3.reshape(t, HALF_D), w_router[l], b_router[l], w_gu[l].astype(BF16), b_gu[l],
                           w_down[l].astype(BF16), b_down[l], ln3_g[l], ln3_b[l], alpha)
    return h.reshape(bsz, s, d)
```

```python
import functools

import jax
import jax.numpy as jnp
from jax import lax
from jax.experimental import pallas as pl
from jax.experimental.pallas import tpu as pltpu

F32 = jnp.float32
BF16 = jnp.bfloat16

D_MODEL = 2048
N_MEM = 256
RET_HEADS = 8
RET_QK_DIM = 128
RET_V_DIM = 256
RET_Q = RET_HEADS * RET_QK_DIM
RET_V = RET_HEADS * RET_V_DIM
RET_CHUNK = 128
ROPE_BASE = 10000.0
SSD_D_INNER = 2 * D_MODEL
SSD_HEAD_DIM = 64
SSD_HEADS = SSD_D_INNER // SSD_HEAD_DIM
SSD_GROUPS = 8
SSD_HPG = SSD_HEADS // SSD_GROUPS
SSD_GROUP_W = SSD_D_INNER // SSD_GROUPS
SSD_STATE = 128
SSD_BC = SSD_GROUPS * SSD_STATE
SSD_CONV = 5
SSD_SUB = 64
CONV_CH = SSD_D_INNER + 2 * SSD_BC
XATTN_HEADS = 4
XATTN_HEAD_DIM = D_MODEL // XATTN_HEADS
N_EXPERTS = 32
TOP_K = 4
EXPERT_DIM = D_MODEL // 2
SWIGLU_LIMIT = 7.0
SWIGLU_ALPHA = 1.702
MOE_BLOCK = 256
LN_EPS = 1e-5
NEG_BIG = -1e30

OFF_Q = 0
OFF_K = OFF_Q + RET_Q
OFF_V = OFF_K + RET_Q
OFF_GRET = OFF_V + RET_V
OFF_Z = OFF_GRET + RET_V
OFF_XS = OFF_Z + SSD_D_INNER
OFF_BM = OFF_XS + SSD_D_INNER
OFF_CM = OFF_BM + SSD_BC
OFF_DT = OFF_CM + SSD_BC
OFF_GATE_R = OFF_DT + 2 * SSD_HEADS
OFF_GATE_S = OFF_GATE_R + D_MODEL
IN_WIDTH = OFF_GATE_S + D_MODEL
MAIN_W = IN_WIDTH - 2 * SSD_HEADS
M_GATE_R = OFF_DT
M_GATE_S = OFF_DT + D_MODEL

VMEM_LIMIT_MB = 48


def _cparams(sem, vmem_mb=VMEM_LIMIT_MB):
    return pltpu.CompilerParams(dimension_semantics=sem, vmem_limit_bytes=vmem_mb * 1024 * 1024)


def _dot(a, b):
    return jnp.dot(a, b, preferred_element_type=F32)


def _dot_nt(a, b):
    return lax.dot_general(a, b, (((1,), (1,)), ((), ())), preferred_element_type=F32)


def _dot_tn(a, b):
    return lax.dot_general(a, b, (((0,), (0,)), ((), ())), preferred_element_type=F32)


def _sigmoid(x):
    return 1.0 / (1.0 + jnp.exp(-x))


def _softplus(x):
    return jnp.maximum(x, 0.0) + jnp.log(1.0 + jnp.exp(-jnp.abs(x)))


def _split2(a):
    a1 = a.astype(BF16)
    return a1, (a - a1.astype(F32)).astype(BF16)


HALF_D = D_MODEL // 2
HI_MASK = 0xFFFF0000


def _pack_pairs(lo, hi):
    lo_bits = lax.bitcast_convert_type(lo.astype(BF16).astype(F32), jnp.uint32)
    hi_bits = lax.bitcast_convert_type(hi.astype(BF16).astype(F32), jnp.uint32)
    return (lo_bits >> 16) | (hi_bits & jnp.uint32(HI_MASK))


def _unpack_pairs(u):
    lo = lax.bitcast_convert_type(u << 16, F32)
    hi = lax.bitcast_convert_type(u & jnp.uint32(HI_MASK), F32)
    return lo, hi


def _ln_rows(x, g, b):
    mu = jnp.mean(x, axis=-1, keepdims=True)
    xc = x - mu
    var = jnp.mean(xc * xc, axis=-1, keepdims=True)
    return xc * lax.rsqrt(var + LN_EPS) * g + b


def _ln_kernel(x_ref, g_ref, b_ref, o_ref, ob_ref):
    y = _ln_rows(x_ref[...], g_ref[...], b_ref[...])
    o_ref[...] = y
    ob_ref[...] = y.astype(BF16)


def _layernorm(x, g, b):
    m, d = x.shape
    tm = min(512, m)
    return pl.pallas_call(
        _ln_kernel,
        grid=(m // tm,),
        in_specs=[pl.BlockSpec((tm, d), lambda i: (i, 0)),
                  pl.BlockSpec((1, d), lambda i: (0, 0)),
                  pl.BlockSpec((1, d), lambda i: (0, 0))],
        out_specs=[pl.BlockSpec((tm, d), lambda i: (i, 0)),
                   pl.BlockSpec((tm, d), lambda i: (i, 0))],
        out_shape=[jax.ShapeDtypeStruct((m, d), F32), jax.ShapeDtypeStruct((m, d), BF16)],
        compiler_params=_cparams(("parallel",)),
        name="layernorm",
    )(x, g.reshape(1, d), b.reshape(1, d))


def _mm_kernel(x_ref, w_ref, o_ref):
    o_ref[...] = _dot(x_ref[...], w_ref[...]).astype(o_ref.dtype)


def _matmul(x, w, out_dtype, tm, tn, name):
    m, k = x.shape
    n = w.shape[1]
    tm = min(tm, m)
    tn = min(tn, n)
    return pl.pallas_call(
        _mm_kernel,
        grid=(m // tm, n // tn),
        in_specs=[pl.BlockSpec((tm, k), lambda i, j: (i, 0)),
                  pl.BlockSpec((k, tn), lambda i, j: (0, j))],
        out_specs=pl.BlockSpec((tm, tn), lambda i, j: (i, j)),
        out_shape=jax.ShapeDtypeStruct((m, n), out_dtype),
        compiler_params=_cparams(("parallel", "arbitrary")),
        name=name,
    )(x, w)


def _in_proj(hb, w):
    m, k = hb.shape
    tm = min(1024, m)
    tn = 1024
    nmain = OFF_DT // tn
    dtw = 2 * SSD_HEADS
    proj = pl.pallas_call(
        _mm_kernel,
        grid=(m // tm, MAIN_W // tn),
        in_specs=[pl.BlockSpec((tm, k), lambda i, j: (i, 0)),
                  pl.BlockSpec((pl.Element(k), pl.Element(tn)),
                               lambda i, j: (0, pl.multiple_of(jnp.where(j < nmain, j * tn, j * tn + dtw), dtw)))],
        out_specs=pl.BlockSpec((tm, tn), lambda i, j: (i, j)),
        out_shape=jax.ShapeDtypeStruct((m, MAIN_W), BF16),
        compiler_params=_cparams(("parallel", "arbitrary")),
        name="in_proj",
    )(hb, w)
    dtraw = pl.pallas_call(
        _mm_kernel,
        grid=(m // tm,),
        in_specs=[pl.BlockSpec((tm, k), lambda i: (i, 0)),
                  pl.BlockSpec((k, dtw), lambda i: (0, OFF_DT // dtw))],
        out_specs=pl.BlockSpec((tm, dtw), lambda i: (i, 0)),
        out_shape=jax.ShapeDtypeStruct((m, dtw), F32),
        compiler_params=_cparams(("parallel",)),
        name="dt_proj",
    )(hb, w)
    return proj, dtraw


def _rope_kernel(pos_ref, inv_ref, sgn_ref, cos_ref, sin_ref):
    ang = pos_ref[...] * inv_ref[...]
    cos_ref[...] = jnp.cos(ang)
    sin_ref[...] = jnp.sin(ang) * sgn_ref[...]


def _rope_tables(positions):
    t = positions.size
    half = RET_QK_DIM // 2
    inv = ROPE_BASE ** (-jnp.arange(half, dtype=F32) / half)
    inv2 = jnp.concatenate([inv, inv]).reshape(1, RET_QK_DIM)
    sgn = jnp.concatenate([-jnp.ones((half,), F32), jnp.ones((half,), F32)]).reshape(1, RET_QK_DIM)
    pos = positions.astype(F32).reshape(t, 1)
    tm = min(1024, t)
    return pl.pallas_call(
        _rope_kernel,
        grid=(t // tm,),
        in_specs=[pl.BlockSpec((tm, 1), lambda i: (i, 0)),
                  pl.BlockSpec((1, RET_QK_DIM), lambda i: (0, 0)),
                  pl.BlockSpec((1, RET_QK_DIM), lambda i: (0, 0))],
        out_specs=[pl.BlockSpec((tm, RET_QK_DIM), lambda i: (i, 0)),
                   pl.BlockSpec((tm, RET_QK_DIM), lambda i: (i, 0))],
        out_shape=[jax.ShapeDtypeStruct((t, RET_QK_DIM), F32)] * 2,
        compiler_params=_cparams(("parallel",)),
        name="rope_tables",
    )(pos, inv2, sgn)


def _ret_kernel(*refs, reverse, nchunk):
    if not reverse:
        (q_ref, k_ref, v_ref, cos_ref, sin_ref, qd_ref, kd_ref, cd_ref, dm_ref, y_ref, st_ref) = refs
    else:
        (q_ref, k_ref, v_ref, cos_ref, sin_ref, qd_ref, kd_ref, cd_ref, yf_ref, g_ref, o_ref, st_ref) = refs

    @pl.when(pl.program_id(2) == 0)
    def _():
        st_ref[...] = jnp.zeros_like(st_ref)

    qd = qd_ref[...]
    kd = kd_ref[...]
    cd = cd_ref[...]
    scale = RET_QK_DIM ** -0.5
    c = RET_CHUNK
    order = range(nchunk - 1, -1, -1) if reverse else range(nchunk)
    st = st_ref[...]
    for ci in order:
        sl = slice(ci * c, (ci + 1) * c)
        cs = cos_ref[sl, :]
        sn = sin_ref[sl, :]
        q = q_ref[sl, :].astype(F32)
        k = k_ref[sl, :].astype(F32)
        v = v_ref[sl, :]
        qr = q * cs + pltpu.roll(q, RET_QK_DIM // 2, 1) * sn
        kr = (k * cs + pltpu.roll(k, RET_QK_DIM // 2, 1) * sn) * scale
        y = _dot((qr * qd).astype(BF16), st.astype(BF16))
        st = cd * st + _dot_tn((kr * kd).astype(BF16), v)
        if not reverse:
            s = _dot_nt(qr.astype(BF16), kr.astype(BF16)) * dm_ref[...]
            y_ref[sl, :] = y + _dot(s.astype(BF16), v)
        else:
            tot = yf_ref[sl, :] + y
            mu = jnp.mean(tot, axis=-1, keepdims=True)
            tc = tot - mu
            var = jnp.mean(tc * tc, axis=-1, keepdims=True)
            g = g_ref[sl, :].astype(F32)
            o_ref[sl, :] = (tc * lax.rsqrt(var + LN_EPS) * (g * _sigmoid(g))).astype(BF16)
    st_ref[...] = st


def _ret_decay_tables():
    lg = jnp.log1p(-jnp.exp2(-5.0 - jnp.arange(RET_HEADS, dtype=F32)))
    idx = jnp.arange(RET_CHUNK, dtype=F32)
    c = float(RET_CHUNK)
    dist = jnp.abs(idx[:, None] - idx[None, :])
    dmat = jnp.exp(lg[:, None, None] * dist)

    def rows(e):
        return jnp.broadcast_to(jnp.exp(lg[:, None] * e)[..., None], (RET_HEADS, RET_CHUNK, RET_QK_DIM))

    qd_f = rows(idx + 1.0)
    kd_f = rows(c - 1.0 - idx)
    qd_b = rows(c - idx)
    kd_b = rows(idx)
    cd = jnp.broadcast_to(jnp.exp(lg * c)[:, None, None], (RET_HEADS, 1, RET_V_DIM))
    return dmat, (qd_f, kd_f), (qd_b, kd_b), cd


def _retention(proj3, cos, sin):
    b, s, _ = proj3.shape
    rb = min(2048, s)
    nb = s // rb
    nchunk = rb // RET_CHUNK
    dmat, dec_f, dec_b, cd = _ret_decay_tables()
    cos3 = cos.reshape(b, s, RET_QK_DIM)
    sin3 = sin.reshape(b, s, RET_QK_DIM)
    kq = OFF_K // RET_QK_DIM
    kv = OFF_V // RET_V_DIM
    kg = OFF_GRET // RET_V_DIM
    tab = lambda w: pl.BlockSpec((None, RET_CHUNK, w), lambda bi, h, i: (h, 0, 0))
    cd_spec = pl.BlockSpec((None, 1, RET_V_DIM), lambda bi, h, i: (h, 0, 0))

    def specs(rev):
        blk = (lambda i: nb - 1 - i) if rev else (lambda i: i)
        return [
            pl.BlockSpec((None, rb, RET_QK_DIM), lambda bi, h, i: (bi, blk(i), h)),
            pl.BlockSpec((None, rb, RET_QK_DIM), lambda bi, h, i: (bi, blk(i), kq + h)),
            pl.BlockSpec((None, rb, RET_V_DIM), lambda bi, h, i: (bi, blk(i), kv + h)),
            pl.BlockSpec((None, rb, RET_QK_DIM), lambda bi, h, i: (bi, blk(i), 0)),
            pl.BlockSpec((None, rb, RET_QK_DIM), lambda bi, h, i: (bi, blk(i), 0)),
            tab(RET_QK_DIM), tab(RET_QK_DIM), cd_spec,
        ], blk

    sem = ("parallel", "parallel", "arbitrary")
    scratch = [pltpu.VMEM((RET_QK_DIM, RET_V_DIM), F32)]
    in_f, _ = specs(False)
    y_f = pl.pallas_call(
        functools.partial(_ret_kernel, reverse=False, nchunk=nchunk),
        grid=(b, RET_HEADS, nb),
        in_specs=in_f + [tab(RET_CHUNK)],
        out_specs=pl.BlockSpec((None, rb, RET_V_DIM), lambda bi, h, i: (bi, i, h)),
        out_shape=jax.ShapeDtypeStruct((b, s, RET_V), F32),
        scratch_shapes=scratch,
        compiler_params=_cparams(sem),
        name="retention_fwd",
    )(proj3, proj3, proj3, cos3, sin3, dec_f[0], dec_f[1], cd, dmat)
    in_b, blk = specs(True)
    out = pl.pallas_call(
        functools.partial(_ret_kernel, reverse=True, nchunk=nchunk),
        grid=(b, RET_HEADS, nb),
        in_specs=in_b + [
            pl.BlockSpec((None, rb, RET_V_DIM), lambda bi, h, i: (bi, blk(i), h)),
            pl.BlockSpec((None, rb, RET_V_DIM), lambda bi, h, i: (bi, blk(i), kg + h)),
        ],
        out_specs=pl.BlockSpec((None, rb, RET_V_DIM), lambda bi, h, i: (bi, blk(i), h)),
        out_shape=jax.ShapeDtypeStruct((b, s, RET_V), BF16),
        scratch_shapes=scratch,
        compiler_params=_cparams(sem),
        name="retention_bwd",
    )(proj3, proj3, proj3, cos3, sin3, dec_b[0], dec_b[1], cd, y_f, proj3)
    return out


CONV_HALO = 16


def _conv_kernel(xm_ref, xp_ref, xn_ref, w_ref, b_ref, o_ref):
    i = pl.program_id(1)
    n = pl.num_programs(1)
    tm = xm_ref.shape[0]
    pad = SSD_CONV // 2
    ext = jnp.concatenate([jnp.where(i == 0, 0.0, xp_ref[...].astype(F32)),
                           xm_ref[...].astype(F32),
                           jnp.where(i == n - 1, 0.0, xn_ref[...].astype(F32))], axis=0)
    rows = tm + 2 * CONV_HALO
    acc = b_ref[...] + w_ref[pad:pad + 1, :] * ext[CONV_HALO:CONV_HALO + tm, :]
    for k in range(SSD_CONV):
        if k != pad:
            sh = pltpu.roll(ext, (pad - k) % rows, 0)
            acc = acc + w_ref[k:k + 1, :] * sh[CONV_HALO:CONV_HALO + tm, :]
    o_ref[...] = (acc * _sigmoid(acc)).astype(BF16)


def _conv_silu(proj3, conv_w, conv_b):
    b, s, _ = proj3.shape
    tm = min(512, s)
    tc = 512
    ns = s // tm
    c0 = OFF_XS // tc
    hb = tm // CONV_HALO
    last = s // CONV_HALO - 1
    return pl.pallas_call(
        _conv_kernel,
        grid=(b, ns, CONV_CH // tc),
        in_specs=[
            pl.BlockSpec((None, tm, tc), lambda bi, i, j: (bi, i, c0 + j)),
            pl.BlockSpec((None, CONV_HALO, tc), lambda bi, i, j: (bi, jnp.maximum(i * hb - 1, 0), c0 + j)),
            pl.BlockSpec((None, CONV_HALO, tc), lambda bi, i, j: (bi, jnp.minimum((i + 1) * hb, last), c0 + j)),
            pl.BlockSpec((SSD_CONV, tc), lambda bi, i, j: (0, j)),
            pl.BlockSpec((1, tc), lambda bi, i, j: (0, j)),
        ],
        out_specs=pl.BlockSpec((None, tm, tc), lambda bi, i, j: (bi, i, j)),
        out_shape=jax.ShapeDtypeStruct((b, s, CONV_CH), BF16),
        compiler_params=_cparams(("parallel", "parallel", "parallel")),
        name="conv_silu",
    )(proj3, proj3, proj3, conv_w, conv_b.reshape(1, CONV_CH))


def _ssd_kernel(*refs, reverse, nchunk):
    if not reverse:
        (x_ref, b_ref, c_ref, dt_ref, bias_ref, alog_ref, y_ref, h_ref) = refs
    else:
        (x_ref, b_ref, c_ref, dt_ref, bias_ref, alog_ref,
         yf_ref, z_ref, dsk_ref, ng_ref, o_ref, h_ref) = refs

    @pl.when(pl.program_id(2) == 0)
    def _():
        h_ref[...] = jnp.zeros_like(h_ref)

    g = pl.program_id(1)
    sub = SSD_SUB
    ch = 2 * sub
    gw = SSD_GROUP_W
    base = (SSD_HEADS if reverse else 0) + g * SSD_HPG
    i32 = jnp.int32
    ci = lax.broadcasted_iota(i32, (ch, gw), 0)
    li = lax.broadcasted_iota(i32, (ch, gw), 1)
    selw = (ci == base + (li >> 6)).astype(BF16)
    rn = lax.broadcasted_iota(i32, (16, ch), 0)
    cn = lax.broadcasted_iota(i32, (16, ch), 1)
    seln = jnp.logical_and(cn == base + rn, rn < SSD_HPG).astype(BF16)
    ii = lax.broadcasted_iota(i32, (ch, ch), 0)
    jj = lax.broadcasted_iota(i32, (ch, ch), 1)
    tri = ((jj >= ii) if reverse else (jj <= ii)).astype(BF16)
    jh = jj & (sub - 1)
    bmask = (ii >> 6) == (jj >> 6)
    if reverse:
        mask_full = (jh + sub) > ii
        mask_half = (jh > ii)[:sub]
    else:
        mask_full = ii >= jh
        mask_half = (ii >= jh)[:sub]
    bias = bias_ref[...]
    a_neg = -jnp.exp(alog_ref[...])

    hst = h_ref[...]
    for t in range(nchunk):
        sc = (nchunk - 1 - t) if reverse else t
        rows = slice(sc * ch, (sc + 1) * ch)
        dt_all = _softplus(dt_ref[rows, :] + bias)
        p_all = sum(_dot(tri, s) for s in _split2(dt_all * a_neg))
        ps = _split2(p_all)
        dtw = sum(_dot(s, selw) for s in _split2(dt_all))
        pw = sum(_dot(s, selw) for s in ps)
        pn = sum(_dot_nt(seln, s) for s in ps)
        x = x_ref[rows, :].astype(F32)
        xdt = x * dtw
        bm = b_ref[rows, :]
        cm = c_ref[rows, :]
        bm_a = jnp.concatenate([bm[:sub], bm[:sub]], axis=0)
        bm_b = jnp.concatenate([bm[sub:], bm[sub:]], axis=0)
        if reverse:
            cb_full = _dot_nt(cm, bm_b)
            cb_half = _dot_nt(cm[:sub], bm_a)
        else:
            cb_full = _dot_nt(cm, bm_a)
            cb_half = _dot_nt(cm[sub:], bm_b)
        parts = []
        for p in range(SSD_HPG // 2):
            colp = pw[:, p * ch:(p + 1) * ch]
            row_a = jnp.concatenate([pn[2 * p:2 * p + 1, :sub], pn[2 * p + 1:2 * p + 2, :sub]], axis=1)
            row_b = jnp.concatenate([pn[2 * p:2 * p + 1, sub:], pn[2 * p + 1:2 * p + 2, sub:]], axis=1)
            xp = xdt[:, p * ch:(p + 1) * ch].astype(BF16)
            xb_a = jnp.where(bmask, jnp.concatenate([xp[:sub], xp[:sub]], axis=0), 0)
            xb_b = jnp.where(bmask, jnp.concatenate([xp[sub:], xp[sub:]], axis=0), 0)
            if reverse:
                w_full = (cb_full * jnp.exp(jnp.where(mask_full, colp - row_b, NEG_BIG))).astype(BF16)
                w_half = (cb_half * jnp.exp(jnp.where(mask_half, colp[:sub] - row_a, NEG_BIG))).astype(BF16)
                y_full = _dot(w_full, xb_b)
                y_half = _dot(w_half, xb_a)
                parts.append(jnp.concatenate([y_full[:sub] + y_half, y_full[sub:]], axis=0))
            else:
                w_full = (cb_full * jnp.exp(jnp.where(mask_full, colp - row_a, NEG_BIG))).astype(BF16)
                w_half = (cb_half * jnp.exp(jnp.where(mask_half, colp[sub:] - row_b, NEG_BIG))).astype(BF16)
                y_full = _dot(w_full, xb_a)
                y_half = _dot(w_half, xb_b)
                parts.append(jnp.concatenate([y_full[:sub], y_full[sub:] + y_half], axis=0))
        y = jnp.concatenate(parts, axis=1) + _dot(cm, hst.astype(BF16)) * jnp.exp(pw)
        plast = pw[0:1, :] if reverse else pw[ch - 1:ch, :]
        xdec = (jnp.exp(plast - pw) * xdt).astype(BF16)
        hst = jnp.exp(plast) * hst + _dot_tn(bm, xdec)
        if not reverse:
            y_ref[rows, :] = y
        else:
            tot = yf_ref[rows, :] + y + dsk_ref[...] * x
            z = z_ref[rows, :].astype(F32)
            tot = tot * (z * _sigmoid(z))
            ms = jnp.mean(tot * tot, axis=-1, keepdims=True)
            o_ref[rows, :] = (tot * lax.rsqrt(ms + LN_EPS) * ng_ref[...]).astype(BF16)
    h_ref[...] = hst


def _ssd(xbc, proj3, dtraw3, dt_bias, a_log, d_skip, norm_g):
    b, s, _ = xbc.shape
    rb = min(1024, s)
    nb = s // rb
    nchunk = rb // (2 * SSD_SUB)
    gw = SSD_GROUP_W
    bias = dt_bias.astype(F32).reshape(1, 2 * SSD_HEADS)
    alog = a_log.astype(F32).reshape(1, 2 * SSD_HEADS)
    dsk = jnp.broadcast_to(d_skip.astype(F32)[:, None], (SSD_HEADS, SSD_HEAD_DIM)).reshape(SSD_GROUPS, 1, gw)
    ng = norm_g.astype(F32).reshape(SSD_GROUPS, 1, gw)
    kb = SSD_D_INNER // SSD_STATE
    kc = (SSD_D_INNER + SSD_BC) // SSD_STATE
    kz = OFF_Z // gw

    def specs(rev):
        blk = (lambda i: nb - 1 - i) if rev else (lambda i: i)
        small = pl.BlockSpec((1, 2 * SSD_HEADS), lambda bi, g, i: (0, 0))
        return [
            pl.BlockSpec((None, rb, gw), lambda bi, g, i: (bi, blk(i), g)),
            pl.BlockSpec((None, rb, SSD_STATE), lambda bi, g, i: (bi, blk(i), kb + g)),
            pl.BlockSpec((None, rb, SSD_STATE), lambda bi, g, i: (bi, blk(i), kc + g)),
            pl.BlockSpec((None, rb, 2 * SSD_HEADS), lambda bi, g, i: (bi, blk(i), 0)),
            small, small,
        ], blk

    sem = ("parallel", "parallel", "arbitrary")
    scratch = [pltpu.VMEM((SSD_STATE, gw), F32)]
    in_f, _ = specs(False)
    y_f = pl.pallas_call(
        functools.partial(_ssd_kernel, reverse=False, nchunk=nchunk),
        grid=(b, SSD_GROUPS, nb),
        in_specs=in_f,
        out_specs=pl.BlockSpec((None, rb, gw), lambda bi, g, i: (bi, i, g)),
        out_shape=jax.ShapeDtypeStruct((b, s, SSD_D_INNER), F32),
        scratch_shapes=scratch,
        compiler_params=_cparams(sem),
        name="ssd_fwd",
    )(xbc, xbc, xbc, dtraw3, bias, alog)
    in_b, blk = specs(True)
    grp = pl.BlockSpec((None, 1, gw), lambda bi, g, i: (g, 0, 0))
    out = pl.pallas_call(
        functools.partial(_ssd_kernel, reverse=True, nchunk=nchunk),
        grid=(b, SSD_GROUPS, nb),
        in_specs=in_b + [
            pl.BlockSpec((None, rb, gw), lambda bi, g, i: (bi, blk(i), g)),
            pl.BlockSpec((None, rb, gw), lambda bi, g, i: (bi, blk(i), kz + g)),
            grp, grp,
        ],
        out_specs=pl.BlockSpec((None, rb, gw), lambda bi, g, i: (bi, blk(i), g)),
        out_shape=jax.ShapeDtypeStruct((b, s, SSD_D_INNER), BF16),
        scratch_shapes=scratch,
        compiler_params=_cparams(sem),
        name="ssd_bwd",
    )(xbc, xbc, xbc, dtraw3, bias, alog, y_f, proj3, dsk, ng)
    return out


def _merge_kernel(ret_ref, ssd_ref, wr_ref, ws_ref, gr_ref, gs_ref, o_ref):
    yr = _dot(ret_ref[...], wr_ref[...])
    ys = _dot(ssd_ref[...], ws_ref[...])
    o = _sigmoid(gr_ref[...].astype(F32)) * yr + _sigmoid(gs_ref[...].astype(F32)) * ys
    o_ref[...] = o.astype(BF16)


def _merge(ret, ssd, proj, w_ret_o, w_ssd_o):
    m = ret.shape[0]
    tm = min(1024, m)
    tn = 256
    gr0 = M_GATE_R // tn
    gs0 = M_GATE_S // tn
    return pl.pallas_call(
        _merge_kernel,
        grid=(m // tm, D_MODEL // tn),
        in_specs=[
            pl.BlockSpec((tm, RET_V), lambda i, j: (i, 0)),
            pl.BlockSpec((tm, SSD_D_INNER), lambda i, j: (i, 0)),
            pl.BlockSpec((RET_V, tn), lambda i, j: (0, j)),
            pl.BlockSpec((SSD_D_INNER, tn), lambda i, j: (0, j)),
            pl.BlockSpec((tm, tn), lambda i, j: (i, gr0 + j)),
            pl.BlockSpec((tm, tn), lambda i, j: (i, gs0 + j)),
        ],
        out_specs=pl.BlockSpec((tm, tn), lambda i, j: (i, j)),
        out_shape=jax.ShapeDtypeStruct((m, D_MODEL), BF16),
        compiler_params=_cparams(("parallel", "arbitrary")),
        name="branch_merge",
    )(ret, ssd, w_ret_o, w_ssd_o, proj, proj)


def _proj_ln_kernel(x_ref, w_ref, h_ref, g_ref, b_ref, o_ref, ob_ref, *, alpha):
    y = _dot(x_ref[...], w_ref[...]) + alpha * h_ref[...]
    y = _ln_rows(y, g_ref[...], b_ref[...])
    o_ref[...] = y
    ob_ref[...] = y.astype(BF16)


def _proj_res_ln(x, w, h, g, b, alpha):
    m, k = x.shape
    d = w.shape[1]
    tm = min(256, m)
    row = lambda i: (i, 0)
    fix = lambda i: (0, 0)
    return pl.pallas_call(
        functools.partial(_proj_ln_kernel, alpha=alpha),
        grid=(m // tm,),
        in_specs=[pl.BlockSpec((tm, k), row), pl.BlockSpec((k, d), fix), pl.BlockSpec((tm, d), row),
                  pl.BlockSpec((1, d), fix), pl.BlockSpec((1, d), fix)],
        out_specs=[pl.BlockSpec((tm, d), row), pl.BlockSpec((tm, d), row)],
        out_shape=[jax.ShapeDtypeStruct((m, d), F32), jax.ShapeDtypeStruct((m, d), BF16)],
        compiler_params=_cparams(("parallel",)),
        name="proj_residual_ln",
    )(x, w, h, g.reshape(1, d), b.reshape(1, d))


def _xattn_kernel(q_ref, k_ref, v_ref, wo_ref, h_ref, g_ref, b_ref, o_ref, op_ref, *, alpha):
    scale = XATTN_HEAD_DIM ** -0.5
    outs = []
    for hd in range(XATTN_HEADS):
        sl = slice(hd * XATTN_HEAD_DIM, (hd + 1) * XATTN_HEAD_DIM)
        s = _dot_nt(q_ref[:, sl], k_ref[:, sl]) * scale
        e = jnp.exp(s - jnp.max(s, axis=-1, keepdims=True))
        p = e / jnp.sum(e, axis=-1, keepdims=True)
        outs.append(_dot(p.astype(BF16), v_ref[:, sl]).astype(BF16))
    o = jnp.concatenate(outs, axis=1)
    y = _dot(o, wo_ref[...]) + alpha * h_ref[...]
    y = _ln_rows(y, g_ref[...], b_ref[...])
    o_ref[...] = y
    op_ref[...] = _pack_pairs(y[:, :HALF_D], y[:, HALF_D:])


def _xattn(q3, kv3, w_xo, h3, g, b, alpha):
    bsz, s, d = q3.shape
    tm = min(256, s)
    row = lambda bi, i: (bi, i, 0)
    fix = lambda bi, i: (0, 0)
    return pl.pallas_call(
        functools.partial(_xattn_kernel, alpha=alpha),
        grid=(bsz, s // tm),
        in_specs=[
            pl.BlockSpec((None, tm, d), row),
            pl.BlockSpec((None, N_MEM, d), lambda bi, i: (bi, 0, 0)),
            pl.BlockSpec((None, N_MEM, d), lambda bi, i: (bi, 0, 1)),
            pl.BlockSpec((d, d), fix),
            pl.BlockSpec((None, tm, d), row),
            pl.BlockSpec((1, d), fix), pl.BlockSpec((1, d), fix),
        ],
        out_specs=[pl.BlockSpec((None, tm, d), row), pl.BlockSpec((None, tm, HALF_D), row)],
        out_shape=[jax.ShapeDtypeStruct((bsz, s, d), F32), jax.ShapeDtypeStruct((bsz, s, HALF_D), jnp.uint32)],
        compiler_params=_cparams(("parallel", "parallel")),
        name="memory_xattn",
    )(q3, kv3, kv3, w_xo, h3, g.reshape(1, d), b.reshape(1, d))


ROUTE_LANES = 128


def _router_kernel(h_ref, w_ref, b_ref, idx_ref, wt_ref, cnt_ref):
    lg = jnp.dot(h_ref[...], w_ref[...], preferred_element_type=F32,
                 precision=lax.Precision.HIGHEST) + b_ref[...]
    tm = lg.shape[0]
    lane = lax.broadcasted_iota(jnp.int32, (tm, N_EXPERTS), 1).astype(F32)
    vals, idxs, hits = [], [], []
    for _ in range(TOP_K):
        m = jnp.max(lg, axis=-1, keepdims=True)
        am = jnp.min(jnp.where(lg == m, lane, float(N_EXPERTS)), axis=-1, keepdims=True)
        vals.append(m)
        idxs.append(am.astype(jnp.int32))
        hits.append(lane == am)
        lg = jnp.where(hits[-1], -jnp.inf, lg)
    es = [jnp.exp(v - vals[0]) for v in vals]
    tot = es[0] + es[1] + es[2] + es[3]

    @pl.when(pl.program_id(0) == 0)
    def _():
        cnt_ref[...] = jnp.zeros_like(cnt_ref)

    tokhot = sum(h.astype(F32) for h in hits)
    ri = lax.broadcasted_iota(jnp.int32, (tm, tm), 0)
    ci = lax.broadcasted_iota(jnp.int32, (tm, tm), 1)
    before = _dot((ci < ri).astype(BF16), tokhot.astype(BF16)) + cnt_ref[...]
    ranks = [jnp.sum(jnp.where(h, before, 0.0), axis=-1, keepdims=True).astype(jnp.int32) for h in hits]
    cnt_ref[...] = cnt_ref[...] + jnp.sum(tokhot, axis=0, keepdims=True)

    out_lane = lax.broadcasted_iota(jnp.int32, (tm, ROUTE_LANES), 1)
    io = jnp.zeros((tm, ROUTE_LANES), jnp.int32)
    wo = jnp.zeros((tm, ROUTE_LANES), F32)
    for k in range(TOP_K):
        io = jnp.where(out_lane == k, idxs[k], io)
        io = jnp.where(out_lane == TOP_K + k, ranks[k], io)
        wo = jnp.where(out_lane == k, es[k] / tot, wo)
    idx_ref[...] = io
    wt_ref[...] = wo


def _router(h, w_router, b_router):
    m, d = h.shape
    tm = min(512, m)
    row = lambda i: (i, 0)
    fix = lambda i: (0, 0)
    idx, wt, cnt = pl.pallas_call(
        _router_kernel,
        grid=(m // tm,),
        in_specs=[pl.BlockSpec((tm, d), row), pl.BlockSpec((d, N_EXPERTS), fix),
                  pl.BlockSpec((1, N_EXPERTS), fix)],
        out_specs=[pl.BlockSpec((tm, ROUTE_LANES), row), pl.BlockSpec((tm, ROUTE_LANES), row),
                   pl.BlockSpec((1, N_EXPERTS), fix)],
        out_shape=[jax.ShapeDtypeStruct((m, ROUTE_LANES), jnp.int32),
                   jax.ShapeDtypeStruct((m, ROUTE_LANES), F32),
                   jax.ShapeDtypeStruct((1, N_EXPERTS), F32)],
        compiler_params=_cparams(("arbitrary",)),
        name="router_topk",
    )(h, w_router, b_router.reshape(1, N_EXPERTS))
    return idx[:, :TOP_K], idx[:, TOP_K:2 * TOP_K], wt, cnt.reshape(N_EXPERTS).astype(jnp.int32)


def _gather_rows(idx_ref, base, r0, n, src_hbm, dst, sem, priorities):
    for r in range(r0, r0 + n):
        row = idx_ref[base + r]
        pltpu.make_async_copy(src_hbm.at[pl.ds(row, 1), :], dst.at[pl.ds(r, 1), :], sem).start(
            priority=priorities[r % len(priorities)])


def _wait_rows(n, src_hbm, dst, sem):
    pltpu.make_async_copy(src_hbm.at[pl.ds(0, n), :], dst, sem).wait()


def _moe_kernel(be_ref, nu_ref, tok_ref, hp_hbm, wgu_ref, bgu_ref, wd_ref, bd_ref, o_ref, xbuf, sem):
    i = pl.program_id(0)
    nused = nu_ref[0]
    mb = MOE_BLOCK
    prio = (1, 1)

    @pl.when(jnp.logical_and(i == 0, nused > 0))
    def _():
        _gather_rows(tok_ref, 0, 0, mb, hp_hbm, xbuf.at[0], sem.at[0], prio)

    @pl.when(i + 1 < nused)
    def _():
        nxt = (i + 1) % 2
        _gather_rows(tok_ref, (i + 1) * mb, 0, mb, hp_hbm, xbuf.at[nxt], sem.at[nxt], prio)

    @pl.when(i < nused)
    def _():
        slot = i % 2
        _wait_rows(mb, hp_hbm, xbuf.at[slot], sem.at[slot])
        lo, hi = _unpack_pairs(xbuf[slot])
        x = jnp.concatenate([lo.astype(BF16), hi.astype(BF16)], axis=1)
        gu = _dot(x, wgu_ref[...]) + bgu_ref[...]
        gate = jnp.minimum(gu[:, :EXPERT_DIM], SWIGLU_LIMIT)
        up = jnp.clip(gu[:, EXPERT_DIM:], -SWIGLU_LIMIT, SWIGLU_LIMIT)
        act = (up + 1.0) * gate * _sigmoid(gate * SWIGLU_ALPHA)
        ye = _dot(act.astype(BF16), wd_ref[...]) + bd_ref[...]
        o_ref[...] = _pack_pairs(ye[:, :HALF_D], ye[:, HALF_D:])

    @pl.when(i >= nused)
    def _():
        o_ref[...] = jnp.zeros_like(o_ref)


def _moe_experts(hp, block_e, n_used, slot_tok, w_gu, b_gu, w_down, b_down):
    d = D_MODEL
    nblk = block_e.shape[0]
    mb = MOE_BLOCK
    grid_spec = pltpu.PrefetchScalarGridSpec(
        num_scalar_prefetch=3,
        grid=(nblk,),
        in_specs=[
            pl.BlockSpec(memory_space=pl.ANY),
            pl.BlockSpec((None, d, 2 * EXPERT_DIM), lambda i, be, nu, tk: (be[i], 0, 0)),
            pl.BlockSpec((None, 1, 2 * EXPERT_DIM), lambda i, be, nu, tk: (be[i], 0, 0)),
            pl.BlockSpec((None, EXPERT_DIM, d), lambda i, be, nu, tk: (be[i], 0, 0)),
            pl.BlockSpec((None, 1, d), lambda i, be, nu, tk: (be[i], 0, 0)),
        ],
        out_specs=pl.BlockSpec((mb, HALF_D), lambda i, be, nu, tk: (i, 0)),
        scratch_shapes=[pltpu.VMEM((2, mb, HALF_D), jnp.uint32), pltpu.SemaphoreType.DMA((2,))],
    )
    return pl.pallas_call(
        _moe_kernel,
        grid_spec=grid_spec,
        out_shape=jax.ShapeDtypeStruct((nblk * mb, HALF_D), jnp.uint32),
        compiler_params=_cparams(("arbitrary",)),
        name="moe_experts",
    )(block_e, n_used, slot_tok, hp, w_gu, b_gu.reshape(N_EXPERTS, 1, -1), w_down,
      b_down.reshape(N_EXPERTS, 1, -1))


def _combine_kernel(dest_ref, yb_hbm, wt_ref, h_ref, g_ref, b_ref, o_ref, ob_ref, gbuf, sem, *, alpha, tm):
    i = pl.program_id(0)
    n = pl.num_programs(0)
    rows = TOP_K * tm
    prio = (0, 1)

    @pl.when(i == 0)
    def _():
        _gather_rows(dest_ref, 0, 0, rows, yb_hbm, gbuf.at[0], sem.at[0], prio)

    @pl.when(i + 1 < n)
    def _():
        nxt = (i + 1) % 2
        _gather_rows(dest_ref, (i + 1) * rows, 0, rows, yb_hbm, gbuf.at[nxt], sem.at[nxt], prio)

    slot = i % 2
    _wait_rows(rows, yb_hbm, gbuf.at[slot], sem.at[slot])
    wt = wt_ref[...]
    flo = fhi = None
    for k in range(TOP_K):
        lo, hi = _unpack_pairs(gbuf[slot, k * tm:(k + 1) * tm, :])
        wk = wt[:, k:k + 1]
        flo = wk * lo if flo is None else flo + wk * lo
        fhi = wk * hi if fhi is None else fhi + wk * hi
    ff = jnp.concatenate([flo, fhi], axis=1)
    y = _ln_rows(alpha * h_ref[...] + ff, g_ref[...], b_ref[...])
    o_ref[...] = y
    ob_ref[...] = y.astype(BF16)


def _moe_combine(yb, dest_blk, top_w, h, g, b, alpha, tm):
    t, d = h.shape
    row = lambda i, ds: (i, 0)
    fix = lambda i, ds: (0, 0)
    grid_spec = pltpu.PrefetchScalarGridSpec(
        num_scalar_prefetch=1,
        grid=(t // tm,),
        in_specs=[pl.BlockSpec(memory_space=pl.ANY), pl.BlockSpec((tm, ROUTE_LANES), row),
                  pl.BlockSpec((tm, d), row), pl.BlockSpec((1, d), fix), pl.BlockSpec((1, d), fix)],
        out_specs=[pl.BlockSpec((tm, d), row), pl.BlockSpec((tm, d), row)],
        scratch_shapes=[pltpu.VMEM((2, TOP_K * tm, HALF_D), jnp.uint32), pltpu.SemaphoreType.DMA((2,))],
    )
    return pl.pallas_call(
        functools.partial(_combine_kernel, alpha=alpha, tm=tm),
        grid_spec=grid_spec,
        out_shape=[jax.ShapeDtypeStruct((t, d), F32), jax.ShapeDtypeStruct((t, d), BF16)],
        compiler_params=_cparams(("arbitrary",)),
        name="moe_combine",
    )(dest_blk, yb, top_w, h, g.reshape(1, d), b.reshape(1, d))


def _moe_layer(h, hp, w_router, b_router, w_gu, b_gu, w_down, b_down, g, b, alpha):
    t, d = h.shape
    top_idx, rank, top_w, counts = _router(h, w_router, b_router)
    n_assign = t * TOP_K
    padded = (counts + MOE_BLOCK - 1) // MOE_BLOCK * MOE_BLOCK
    pend = jnp.cumsum(padded)
    pstart = pend - padded
    dest = (pstart[top_idx] + rank).astype(jnp.int32)
    n_blocks = -(-n_assign // MOE_BLOCK) + N_EXPERTS
    n_slots = n_blocks * MOE_BLOCK
    tok_flat = jnp.repeat(jnp.arange(t, dtype=jnp.int32), TOP_K)
    slot_tok = jnp.zeros((n_slots,), jnp.int32).at[dest.reshape(-1)].set(
        tok_flat, unique_indices=True, mode="promise_in_bounds")
    block_e = jnp.minimum(
        jnp.searchsorted(pend, jnp.arange(n_blocks, dtype=jnp.int32) * MOE_BLOCK, side="right"),
        N_EXPERTS - 1).astype(jnp.int32)
    n_used = (pend[-1] // MOE_BLOCK).astype(jnp.int32).reshape(1)
    yb = _moe_experts(hp, block_e, n_used, slot_tok, w_gu, b_gu, w_down, b_down)
    tm = min(128, t)
    dest_blk = dest.reshape(t // tm, tm, TOP_K).transpose(0, 2, 1).reshape(-1)
    return _moe_combine(yb, dest_blk, top_w, h, g, b, alpha, tm)


def kernel(x, mem, positions, ln_in_g, ln_in_b, ln_mem_g, ln_mem_b, w_in, conv_w, conv_b, dt_bias, a_log, d_skip, ssd_norm_g, w_ret_o, w_ssd_o, w_mix_o, ln1_g, ln1_b, w_xq, w_xkv, w_xo, ln2_g, ln2_b, w_router, b_router, w_gu, b_gu, w_down, b_down, ln3_g, ln3_b):
    bsz, s, d = x.shape
    t = bsz * s
    depth = w_in.shape[0]
    alpha = (2.0 * depth) ** 0.25
    h, hb = _layernorm(x.reshape(t, d), ln_in_g, ln_in_b)
    _, memb = _layernorm(mem.reshape(bsz * N_MEM, d), ln_mem_g, ln_mem_b)
    cos, sin = _rope_tables(positions)
    for l in range(depth):
        w_in_b = w_in[l].astype(BF16)
        proj, dtraw = _in_proj(hb, w_in_b)
        proj3 = proj.reshape(bsz, s, MAIN_W)
        ret = _retention(proj3, cos, sin)
        xbc = _conv_silu(proj3, conv_w[l], conv_b[l])
        ssd = _ssd(xbc, proj3, dtraw.reshape(bsz, s, 2 * SSD_HEADS), dt_bias[l], a_log[l], d_skip[l],
                   ssd_norm_g[l])
        merged = _merge(ret.reshape(t, RET_V), ssd.reshape(t, SSD_D_INNER), proj,
                        w_ret_o[l].astype(BF16), w_ssd_o[l].astype(BF16))
        h, hb = _proj_res_ln(merged, w_mix_o[l].astype(BF16), h, ln1_g[l], ln1_b[l], alpha)

        q = _matmul(hb, w_xq[l].astype(BF16), BF16, 1024, 1024, "xattn_q")
        kv = _matmul(memb, w_xkv[l].astype(BF16), BF16, 512, 1024, "xattn_kv")
        h3, hp3 = _xattn(q.reshape(bsz, s, d), kv.reshape(bsz, N_MEM, 2 * d), w_xo[l].astype(BF16),
                         h.reshape(bsz, s, d), ln2_g[l], ln2_b[l], alpha)
        h = h3.reshape(t, d)

        h, hb = _moe_layer(h, hp3.reshape(t, HALF_D), w_router[l], b_router[l], w_gu[l].astype(BF16), b_gu[l],
                           w_down[l].astype(BF16), b_down[l], ln3_g[l], ln3_b[l], alpha)
    return h.reshape(bsz, s, d)
```

```python
import functools

import jax
import jax.numpy as jnp
from jax import lax
from jax.experimental import pallas as pl
from jax.experimental.pallas import tpu as pltpu

F32 = jnp.float32
BF16 = jnp.bfloat16

D_MODEL = 2048
N_MEM = 256
RET_HEADS = 8
RET_QK_DIM = 128
RET_V_DIM = 256
RET_Q = RET_HEADS * RET_QK_DIM
RET_V = RET_HEADS * RET_V_DIM
RET_CHUNK = 128
ROPE_BASE = 10000.0
SSD_D_INNER = 2 * D_MODEL
SSD_HEAD_DIM = 64
SSD_HEADS = SSD_D_INNER // SSD_HEAD_DIM
SSD_GROUPS = 8
SSD_HPG = SSD_HEADS // SSD_GROUPS
SSD_GROUP_W = SSD_D_INNER // SSD_GROUPS
SSD_STATE = 128
SSD_BC = SSD_GROUPS * SSD_STATE
SSD_CONV = 5
SSD_SUB = 64
CONV_CH = SSD_D_INNER + 2 * SSD_BC
XATTN_HEADS = 4
XATTN_HEAD_DIM = D_MODEL // XATTN_HEADS
N_EXPERTS = 32
TOP_K = 4
EXPERT_DIM = D_MODEL // 2
SWIGLU_LIMIT = 7.0
SWIGLU_ALPHA = 1.702
MOE_BLOCK = 256
LN_EPS = 1e-5
NEG_BIG = -1e30

OFF_Q = 0
OFF_K = OFF_Q + RET_Q
OFF_V = OFF_K + RET_Q
OFF_GRET = OFF_V + RET_V
OFF_Z = OFF_GRET + RET_V
OFF_XS = OFF_Z + SSD_D_INNER
OFF_BM = OFF_XS + SSD_D_INNER
OFF_CM = OFF_BM + SSD_BC
OFF_DT = OFF_CM + SSD_BC
OFF_GATE_R = OFF_DT + 2 * SSD_HEADS
OFF_GATE_S = OFF_GATE_R + D_MODEL
IN_WIDTH = OFF_GATE_S + D_MODEL
MAIN_W = IN_WIDTH - 2 * SSD_HEADS
M_GATE_R = OFF_DT
M_GATE_S = OFF_DT + D_MODEL

VMEM_LIMIT_MB = 48


def _cparams(sem, vmem_mb=VMEM_LIMIT_MB):
    return pltpu.CompilerParams(dimension_semantics=sem, vmem_limit_bytes=vmem_mb * 1024 * 1024)


def _dot(a, b):
    return jnp.dot(a, b, preferred_element_type=F32)


def _dot_nt(a, b):
    return lax.dot_general(a, b, (((1,), (1,)), ((), ())), preferred_element_type=F32)


def _dot_tn(a, b):
    return lax.dot_general(a, b, (((0,), (0,)), ((), ())), preferred_element_type=F32)


def _sigmoid(x):
    return 1.0 / (1.0 + jnp.exp(-x))


def _softplus(x):
    return jnp.maximum(x, 0.0) + jnp.log(1.0 + jnp.exp(-jnp.abs(x)))


def _split2(a):
    a1 = a.astype(BF16)
    return a1, (a - a1.astype(F32)).astype(BF16)


HALF_D = D_MODEL // 2
HI_MASK = 0xFFFF0000


def _pack_pairs(lo, hi):
    lo_bits = lax.bitcast_convert_type(lo.astype(BF16).astype(F32), jnp.uint32)
    hi_bits = lax.bitcast_convert_type(hi.astype(BF16).astype(F32), jnp.uint32)
    return (lo_bits >> 16) | (hi_bits & jnp.uint32(HI_MASK))


def _unpack_pairs(u):
    lo = lax.bitcast_convert_type(u << 16, F32)
    hi = lax.bitcast_convert_type(u & jnp.uint32(HI_MASK), F32)
    return lo, hi


def _ln_rows(x, g, b):
    mu = jnp.mean(x, axis=-1, keepdims=True)
    xc = x - mu
    var = jnp.mean(xc * xc, axis=-1, keepdims=True)
    return xc * lax.rsqrt(var + LN_EPS) * g + b


def _ln_kernel(x_ref, g_ref, b_ref, o_ref, ob_ref):
    y = _ln_rows(x_ref[...], g_ref[...], b_ref[...])
    o_ref[...] = y
    ob_ref[...] = y.astype(BF16)


def _layernorm(x, g, b):
    m, d = x.shape
    tm = min(512, m)
    return pl.pallas_call(
        _ln_kernel,
        grid=(m // tm,),
        in_specs=[pl.BlockSpec((tm, d), lambda i: (i, 0)),
                  pl.BlockSpec((1, d), lambda i: (0, 0)),
                  pl.BlockSpec((1, d), lambda i: (0, 0))],
        out_specs=[pl.BlockSpec((tm, d), lambda i: (i, 0)),
                   pl.BlockSpec((tm, d), lambda i: (i, 0))],
        out_shape=[jax.ShapeDtypeStruct((m, d), F32), jax.ShapeDtypeStruct((m, d), BF16)],
        compiler_params=_cparams(("parallel",)),
        name="layernorm",
    )(x, g.reshape(1, d), b.reshape(1, d))


def _mm_kernel(x_ref, w_ref, o_ref):
    o_ref[...] = _dot(x_ref[...], w_ref[...]).astype(o_ref.dtype)


def _matmul(x, w, layer, out_dtype, tm, tn, name):
    m, k = x.shape
    n = w.shape[2]
    tm = min(tm, m)
    tn = min(tn, n)
    return pl.pallas_call(
        _mm_kernel,
        grid=(m // tm, n // tn),
        in_specs=[pl.BlockSpec((tm, k), lambda i, j: (i, 0)),
                  pl.BlockSpec((None, k, tn), lambda i, j: (layer, 0, j))],
        out_specs=pl.BlockSpec((tm, tn), lambda i, j: (i, j)),
        out_shape=jax.ShapeDtypeStruct((m, n), out_dtype),
        compiler_params=_cparams(("parallel", "arbitrary")),
        name=name,
    )(x, w)


def _mm_lead_kernel(x_ref, w_ref, o_ref):
    o_ref[...] = _dot(x_ref[...], w_ref[0]).astype(o_ref.dtype)


def _in_proj(hb, w, layer):
    m, k = hb.shape
    tm = min(1024, m)
    tn = 1024
    nmain = OFF_DT // tn
    dtw = 2 * SSD_HEADS
    proj = pl.pallas_call(
        _mm_lead_kernel,
        grid=(m // tm, MAIN_W // tn),
        in_specs=[pl.BlockSpec((tm, k), lambda i, j: (i, 0)),
                  pl.BlockSpec((pl.Element(1), pl.Element(k), pl.Element(tn)),
                               lambda i, j: (layer, 0,
                                             pl.multiple_of(jnp.where(j < nmain, j * tn, j * tn + dtw), dtw)))],
        out_specs=pl.BlockSpec((tm, tn), lambda i, j: (i, j)),
        out_shape=jax.ShapeDtypeStruct((m, MAIN_W), BF16),
        compiler_params=_cparams(("parallel", "arbitrary")),
        name="in_proj",
    )(hb, w)
    dtraw = pl.pallas_call(
        _mm_kernel,
        grid=(m // tm,),
        in_specs=[pl.BlockSpec((tm, k), lambda i: (i, 0)),
                  pl.BlockSpec((None, k, dtw), lambda i: (layer, 0, OFF_DT // dtw))],
        out_specs=pl.BlockSpec((tm, dtw), lambda i: (i, 0)),
        out_shape=jax.ShapeDtypeStruct((m, dtw), F32),
        compiler_params=_cparams(("parallel",)),
        name="dt_proj",
    )(hb, w)
    return proj, dtraw


def _rope_kernel(pos_ref, inv_ref, sgn_ref, cos_ref, sin_ref):
    ang = pos_ref[...] * inv_ref[...]
    cos_ref[...] = jnp.cos(ang)
    sin_ref[...] = jnp.sin(ang) * sgn_ref[...]


def _rope_tables(positions):
    t = positions.size
    half = RET_QK_DIM // 2
    inv = ROPE_BASE ** (-jnp.arange(half, dtype=F32) / half)
    inv2 = jnp.concatenate([inv, inv]).reshape(1, RET_QK_DIM)
    sgn = jnp.concatenate([-jnp.ones((half,), F32), jnp.ones((half,), F32)]).reshape(1, RET_QK_DIM)
    pos = positions.astype(F32).reshape(t, 1)
    tm = min(1024, t)
    return pl.pallas_call(
        _rope_kernel,
        grid=(t // tm,),
        in_specs=[pl.BlockSpec((tm, 1), lambda i: (i, 0)),
                  pl.BlockSpec((1, RET_QK_DIM), lambda i: (0, 0)),
                  pl.BlockSpec((1, RET_QK_DIM), lambda i: (0, 0))],
        out_specs=[pl.BlockSpec((tm, RET_QK_DIM), lambda i: (i, 0)),
                   pl.BlockSpec((tm, RET_QK_DIM), lambda i: (i, 0))],
        out_shape=[jax.ShapeDtypeStruct((t, RET_QK_DIM), F32)] * 2,
        compiler_params=_cparams(("parallel",)),
        name="rope_tables",
    )(pos, inv2, sgn)


def _ret_kernel(*refs, reverse, nchunk):
    if not reverse:
        (q_ref, k_ref, v_ref, cos_ref, sin_ref, qd_ref, kd_ref, cd_ref, dm_ref, y_ref, st_ref) = refs
    else:
        (q_ref, k_ref, v_ref, cos_ref, sin_ref, qd_ref, kd_ref, cd_ref, yf_ref, g_ref, o_ref, st_ref) = refs

    @pl.when(pl.program_id(2) == 0)
    def _():
        st_ref[...] = jnp.zeros_like(st_ref)

    qd = qd_ref[...]
    kd = kd_ref[...]
    cd = cd_ref[...]
    scale = RET_QK_DIM ** -0.5
    c = RET_CHUNK
    order = range(nchunk - 1, -1, -1) if reverse else range(nchunk)
    st = st_ref[...]
    for ci in order:
        sl = slice(ci * c, (ci + 1) * c)
        cs = cos_ref[sl, :]
        sn = sin_ref[sl, :]
        q = q_ref[sl, :].astype(F32)
        k = k_ref[sl, :].astype(F32)
        v = v_ref[sl, :]
        qr = q * cs + pltpu.roll(q, RET_QK_DIM // 2, 1) * sn
        kr = (k * cs + pltpu.roll(k, RET_QK_DIM // 2, 1) * sn) * scale
        y = _dot((qr * qd).astype(BF16), st.astype(BF16))
        st = cd * st + _dot_tn((kr * kd).astype(BF16), v)
        if not reverse:
            s = _dot_nt(qr.astype(BF16), kr.astype(BF16)) * dm_ref[...]
            y_ref[sl, :] = y + _dot(s.astype(BF16), v)
        else:
            tot = yf_ref[sl, :] + y
            mu = jnp.mean(tot, axis=-1, keepdims=True)
            tc = tot - mu
            var = jnp.mean(tc * tc, axis=-1, keepdims=True)
            g = g_ref[sl, :].astype(F32)
            o_ref[sl, :] = (tc * lax.rsqrt(var + LN_EPS) * (g * _sigmoid(g))).astype(BF16)
    st_ref[...] = st


def _ret_decay_tables():
    lg = jnp.log1p(-jnp.exp2(-5.0 - jnp.arange(RET_HEADS, dtype=F32)))
    idx = jnp.arange(RET_CHUNK, dtype=F32)
    c = float(RET_CHUNK)
    dist = jnp.abs(idx[:, None] - idx[None, :])
    dmat = jnp.exp(lg[:, None, None] * dist)

    def rows(e):
        return jnp.broadcast_to(jnp.exp(lg[:, None] * e)[..., None], (RET_HEADS, RET_CHUNK, RET_QK_DIM))

    qd_f = rows(idx + 1.0)
    kd_f = rows(c - 1.0 - idx)
    qd_b = rows(c - idx)
    kd_b = rows(idx)
    cd = jnp.broadcast_to(jnp.exp(lg * c)[:, None, None], (RET_HEADS, 1, RET_V_DIM))
    return dmat, (qd_f, kd_f), (qd_b, kd_b), cd


def _retention(proj3, cos, sin):
    b, s, _ = proj3.shape
    rb = min(2048, s)
    nb = s // rb
    nchunk = rb // RET_CHUNK
    dmat, dec_f, dec_b, cd = _ret_decay_tables()
    cos3 = cos.reshape(b, s, RET_QK_DIM)
    sin3 = sin.reshape(b, s, RET_QK_DIM)
    kq = OFF_K // RET_QK_DIM
    kv = OFF_V // RET_V_DIM
    kg = OFF_GRET // RET_V_DIM
    tab = lambda w: pl.BlockSpec((None, RET_CHUNK, w), lambda bi, h, i: (h, 0, 0))
    cd_spec = pl.BlockSpec((None, 1, RET_V_DIM), lambda bi, h, i: (h, 0, 0))

    def specs(rev):
        blk = (lambda i: nb - 1 - i) if rev else (lambda i: i)
        return [
            pl.BlockSpec((None, rb, RET_QK_DIM), lambda bi, h, i: (bi, blk(i), h)),
            pl.BlockSpec((None, rb, RET_QK_DIM), lambda bi, h, i: (bi, blk(i), kq + h)),
            pl.BlockSpec((None, rb, RET_V_DIM), lambda bi, h, i: (bi, blk(i), kv + h)),
            pl.BlockSpec((None, rb, RET_QK_DIM), lambda bi, h, i: (bi, blk(i), 0)),
            pl.BlockSpec((None, rb, RET_QK_DIM), lambda bi, h, i: (bi, blk(i), 0)),
            tab(RET_QK_DIM), tab(RET_QK_DIM), cd_spec,
        ], blk

    sem = ("parallel", "parallel", "arbitrary")
    scratch = [pltpu.VMEM((RET_QK_DIM, RET_V_DIM), F32)]
    in_f, _ = specs(False)
    y_f = pl.pallas_call(
        functools.partial(_ret_kernel, reverse=False, nchunk=nchunk),
        grid=(b, RET_HEADS, nb),
        in_specs=in_f + [tab(RET_CHUNK)],
        out_specs=pl.BlockSpec((None, rb, RET_V_DIM), lambda bi, h, i: (bi, i, h)),
        out_shape=jax.ShapeDtypeStruct((b, s, RET_V), F32),
        scratch_shapes=scratch,
        compiler_params=_cparams(sem),
        name="retention_fwd",
    )(proj3, proj3, proj3, cos3, sin3, dec_f[0], dec_f[1], cd, dmat)
    in_b, blk = specs(True)
    out = pl.pallas_call(
        functools.partial(_ret_kernel, reverse=True, nchunk=nchunk),
        grid=(b, RET_HEADS, nb),
        in_specs=in_b + [
            pl.BlockSpec((None, rb, RET_V_DIM), lambda bi, h, i: (bi, blk(i), h)),
            pl.BlockSpec((None, rb, RET_V_DIM), lambda bi, h, i: (bi, blk(i), kg + h)),
        ],
        out_specs=pl.BlockSpec((None, rb, RET_V_DIM), lambda bi, h, i: (bi, blk(i), h)),
        out_shape=jax.ShapeDtypeStruct((b, s, RET_V), BF16),
        scratch_shapes=scratch,
        compiler_params=_cparams(sem),
        name="retention_bwd",
    )(proj3, proj3, proj3, cos3, sin3, dec_b[0], dec_b[1], cd, y_f, proj3)
    return out


CONV_HALO = 16


def _conv_kernel(xm_ref, xp_ref, xn_ref, w_ref, b_ref, o_ref):
    i = pl.program_id(1)
    n = pl.num_programs(1)
    tm = xm_ref.shape[0]
    pad = SSD_CONV // 2
    ext = jnp.concatenate([jnp.where(i == 0, 0.0, xp_ref[...].astype(F32)),
                           xm_ref[...].astype(F32),
                           jnp.where(i == n - 1, 0.0, xn_ref[...].astype(F32))], axis=0)
    rows = tm + 2 * CONV_HALO
    acc = b_ref[...] + w_ref[pad:pad + 1, :] * ext[CONV_HALO:CONV_HALO + tm, :]
    for k in range(SSD_CONV):
        if k != pad:
            sh = pltpu.roll(ext, (pad - k) % rows, 0)
            acc = acc + w_ref[k:k + 1, :] * sh[CONV_HALO:CONV_HALO + tm, :]
    o_ref[...] = (acc * _sigmoid(acc)).astype(BF16)


def _conv_silu(proj3, conv_w, conv_b):
    b, s, _ = proj3.shape
    tm = min(512, s)
    tc = 512
    ns = s // tm
    c0 = OFF_XS // tc
    hb = tm // CONV_HALO
    last = s // CONV_HALO - 1
    return pl.pallas_call(
        _conv_kernel,
        grid=(b, ns, CONV_CH // tc),
        in_specs=[
            pl.BlockSpec((None, tm, tc), lambda bi, i, j: (bi, i, c0 + j)),
            pl.BlockSpec((None, CONV_HALO, tc), lambda bi, i, j: (bi, jnp.maximum(i * hb - 1, 0), c0 + j)),
            pl.BlockSpec((None, CONV_HALO, tc), lambda bi, i, j: (bi, jnp.minimum((i + 1) * hb, last), c0 + j)),
            pl.BlockSpec((SSD_CONV, tc), lambda bi, i, j: (0, j)),
            pl.BlockSpec((1, tc), lambda bi, i, j: (0, j)),
        ],
        out_specs=pl.BlockSpec((None, tm, tc), lambda bi, i, j: (bi, i, j)),
        out_shape=jax.ShapeDtypeStruct((b, s, CONV_CH), BF16),
        compiler_params=_cparams(("parallel", "parallel", "parallel")),
        name="conv_silu",
    )(proj3, proj3, proj3, conv_w, conv_b.reshape(1, CONV_CH))


def _ssd_kernel(*refs, reverse, nchunk):
    if not reverse:
        (x_ref, b_ref, c_ref, dt_ref, bias_ref, alog_ref, y_ref, h_ref) = refs
    else:
        (x_ref, b_ref, c_ref, dt_ref, bias_ref, alog_ref,
         yf_ref, z_ref, dsk_ref, ng_ref, o_ref, h_ref) = refs

    @pl.when(pl.program_id(2) == 0)
    def _():
        h_ref[...] = jnp.zeros_like(h_ref)

    g = pl.program_id(1)
    sub = SSD_SUB
    ch = 2 * sub
    gw = SSD_GROUP_W
    base = (SSD_HEADS if reverse else 0) + g * SSD_HPG
    i32 = jnp.int32
    ci = lax.broadcasted_iota(i32, (ch, gw), 0)
    li = lax.broadcasted_iota(i32, (ch, gw), 1)
    selw = (ci == base + (li >> 6)).astype(BF16)
    rn = lax.broadcasted_iota(i32, (16, ch), 0)
    cn = lax.broadcasted_iota(i32, (16, ch), 1)
    seln = jnp.logical_and(cn == base + rn, rn < SSD_HPG).astype(BF16)
    ii = lax.broadcasted_iota(i32, (ch, ch), 0)
    jj = lax.broadcasted_iota(i32, (ch, ch), 1)
    tri = ((jj >= ii) if reverse else (jj <= ii)).astype(BF16)
    jh = jj & (sub - 1)
    bmask = (ii >> 6) == (jj >> 6)
    if reverse:
        mask_full = (jh + sub) > ii
        mask_half = (jh > ii)[:sub]
    else:
        mask_full = ii >= jh
        mask_half = (ii >= jh)[:sub]
    bias = bias_ref[...]
    a_neg = -jnp.exp(alog_ref[...])

    hst = h_ref[...]
    for t in range(nchunk):
        sc = (nchunk - 1 - t) if reverse else t
        rows = slice(sc * ch, (sc + 1) * ch)
        dt_all = _softplus(dt_ref[rows, :] + bias)
        p_all = sum(_dot(tri, s) for s in _split2(dt_all * a_neg))
        ps = _split2(p_all)
        wide = _dot(jnp.concatenate(_split2(dt_all) + ps, axis=0), selw)
        dtw = wide[:ch] + wide[ch:2 * ch]
        pw = wide[2 * ch:3 * ch] + wide[3 * ch:]
        pn = sum(_dot_nt(seln, s) for s in ps)
        x = x_ref[rows, :].astype(F32)
        xdt = x * dtw
        bm = b_ref[rows, :]
        cm = c_ref[rows, :]
        bm_a = jnp.concatenate([bm[:sub], bm[:sub]], axis=0)
        bm_b = jnp.concatenate([bm[sub:], bm[sub:]], axis=0)
        if reverse:
            cb_full = _dot_nt(cm, bm_b)
            cb_half = _dot_nt(cm[:sub], bm_a)
        else:
            cb_full = _dot_nt(cm, bm_a)
            cb_half = _dot_nt(cm[sub:], bm_b)
        parts = []
        for p in range(SSD_HPG // 2):
            colp = pw[:, p * ch:(p + 1) * ch]
            row_a = jnp.concatenate([pn[2 * p:2 * p + 1, :sub], pn[2 * p + 1:2 * p + 2, :sub]], axis=1)
            row_b = jnp.concatenate([pn[2 * p:2 * p + 1, sub:], pn[2 * p + 1:2 * p + 2, sub:]], axis=1)
            xp = xdt[:, p * ch:(p + 1) * ch].astype(BF16)
            xb_a = jnp.where(bmask, jnp.concatenate([xp[:sub], xp[:sub]], axis=0), 0)
            xb_b = jnp.where(bmask, jnp.concatenate([xp[sub:], xp[sub:]], axis=0), 0)
            if reverse:
                w_full = (cb_full * jnp.exp(jnp.where(mask_full, colp - row_b, NEG_BIG))).astype(BF16)
                w_half = (cb_half * jnp.exp(jnp.where(mask_half, colp[:sub] - row_a, NEG_BIG))).astype(BF16)
                y_full = _dot(w_full, xb_b)
                y_half = _dot(w_half, xb_a)
                parts.append(jnp.concatenate([y_full[:sub] + y_half, y_full[sub:]], axis=0))
            else:
                w_full = (cb_full * jnp.exp(jnp.where(mask_full, colp - row_a, NEG_BIG))).astype(BF16)
                w_half = (cb_half * jnp.exp(jnp.where(mask_half, colp[sub:] - row_b, NEG_BIG))).astype(BF16)
                y_full = _dot(w_full, xb_a)
                y_half = _dot(w_half, xb_b)
                parts.append(jnp.concatenate([y_full[:sub], y_full[sub:] + y_half], axis=0))
        y = jnp.concatenate(parts, axis=1) + _dot(cm, hst.astype(BF16)) * jnp.exp(pw)
        plast = pw[0:1, :] if reverse else pw[ch - 1:ch, :]
        xdec = (jnp.exp(plast - pw) * xdt).astype(BF16)
        hst = jnp.exp(plast) * hst + _dot_tn(bm, xdec)
        if not reverse:
            y_ref[rows, :] = y
        else:
            tot = yf_ref[rows, :] + y + dsk_ref[...] * x
            z = z_ref[rows, :].astype(F32)
            tot = tot * (z * _sigmoid(z))
            ms = jnp.mean(tot * tot, axis=-1, keepdims=True)
            o_ref[rows, :] = (tot * lax.rsqrt(ms + LN_EPS) * ng_ref[...]).astype(BF16)
    h_ref[...] = hst


def _ssd(xbc, proj3, dtraw3, dt_bias, a_log, d_skip, norm_g):
    b, s, _ = xbc.shape
    rb = min(1024, s)
    nb = s // rb
    nchunk = rb // (2 * SSD_SUB)
    gw = SSD_GROUP_W
    bias = dt_bias.astype(F32).reshape(1, 2 * SSD_HEADS)
    alog = a_log.astype(F32).reshape(1, 2 * SSD_HEADS)
    dsk = jnp.broadcast_to(d_skip.astype(F32)[:, None], (SSD_HEADS, SSD_HEAD_DIM)).reshape(SSD_GROUPS, 1, gw)
    ng = norm_g.astype(F32).reshape(SSD_GROUPS, 1, gw)
    kb = SSD_D_INNER // SSD_STATE
    kc = (SSD_D_INNER + SSD_BC) // SSD_STATE
    kz = OFF_Z // gw

    def specs(rev):
        blk = (lambda i: nb - 1 - i) if rev else (lambda i: i)
        small = pl.BlockSpec((1, 2 * SSD_HEADS), lambda bi, g, i: (0, 0))
        return [
            pl.BlockSpec((None, rb, gw), lambda bi, g, i: (bi, blk(i), g)),
            pl.BlockSpec((None, rb, SSD_STATE), lambda bi, g, i: (bi, blk(i), kb + g)),
            pl.BlockSpec((None, rb, SSD_STATE), lambda bi, g, i: (bi, blk(i), kc + g)),
            pl.BlockSpec((None, rb, 2 * SSD_HEADS), lambda bi, g, i: (bi, blk(i), 0)),
            small, small,
        ], blk

    sem = ("parallel", "parallel", "arbitrary")
    scratch = [pltpu.VMEM((SSD_STATE, gw), F32)]
    in_f, _ = specs(False)
    y_f = pl.pallas_call(
        functools.partial(_ssd_kernel, reverse=False, nchunk=nchunk),
        grid=(b, SSD_GROUPS, nb),
        in_specs=in_f,
        out_specs=pl.BlockSpec((None, rb, gw), lambda bi, g, i: (bi, i, g)),
        out_shape=jax.ShapeDtypeStruct((b, s, SSD_D_INNER), F32),
        scratch_shapes=scratch,
        compiler_params=_cparams(sem),
        name="ssd_fwd",
    )(xbc, xbc, xbc, dtraw3, bias, alog)
    in_b, blk = specs(True)
    grp = pl.BlockSpec((None, 1, gw), lambda bi, g, i: (g, 0, 0))
    out = pl.pallas_call(
        functools.partial(_ssd_kernel, reverse=True, nchunk=nchunk),
        grid=(b, SSD_GROUPS, nb),
        in_specs=in_b + [
            pl.BlockSpec((None, rb, gw), lambda bi, g, i: (bi, blk(i), g)),
            pl.BlockSpec((None, rb, gw), lambda bi, g, i: (bi, blk(i), kz + g)),
            grp, grp,
        ],
        out_specs=pl.BlockSpec((None, rb, gw), lambda bi, g, i: (bi, blk(i), g)),
        out_shape=jax.ShapeDtypeStruct((b, s, SSD_D_INNER), BF16),
        scratch_shapes=scratch,
        compiler_params=_cparams(sem),
        name="ssd_bwd",
    )(xbc, xbc, xbc, dtraw3, bias, alog, y_f, proj3, dsk, ng)
    return out


def _merge_kernel(ret_ref, ssd_ref, wr_ref, ws_ref, gr_ref, gs_ref, o_ref):
    yr = _dot(ret_ref[...], wr_ref[...])
    ys = _dot(ssd_ref[...], ws_ref[...])
    o = _sigmoid(gr_ref[...].astype(F32)) * yr + _sigmoid(gs_ref[...].astype(F32)) * ys
    o_ref[...] = o.astype(BF16)


def _merge(ret, ssd, proj, w_ret_o, w_ssd_o, layer):
    m = ret.shape[0]
    tm = min(1024, m)
    tn = 256
    gr0 = M_GATE_R // tn
    gs0 = M_GATE_S // tn
    return pl.pallas_call(
        _merge_kernel,
        grid=(m // tm, D_MODEL // tn),
        in_specs=[
            pl.BlockSpec((tm, RET_V), lambda i, j: (i, 0)),
            pl.BlockSpec((tm, SSD_D_INNER), lambda i, j: (i, 0)),
            pl.BlockSpec((None, RET_V, tn), lambda i, j: (layer, 0, j)),
            pl.BlockSpec((None, SSD_D_INNER, tn), lambda i, j: (layer, 0, j)),
            pl.BlockSpec((tm, tn), lambda i, j: (i, gr0 + j)),
            pl.BlockSpec((tm, tn), lambda i, j: (i, gs0 + j)),
        ],
        out_specs=pl.BlockSpec((tm, tn), lambda i, j: (i, j)),
        out_shape=jax.ShapeDtypeStruct((m, D_MODEL), BF16),
        compiler_params=_cparams(("parallel", "arbitrary")),
        name="branch_merge",
    )(ret, ssd, w_ret_o, w_ssd_o, proj, proj)


def _proj_ln_kernel(x_ref, w_ref, h_ref, g_ref, b_ref, o_ref, ob_ref, *, alpha):
    y = _dot(x_ref[...], w_ref[...]) + alpha * h_ref[...]
    y = _ln_rows(y, g_ref[...], b_ref[...])
    o_ref[...] = y
    ob_ref[...] = y.astype(BF16)


def _proj_res_ln(x, w, layer, h, g, b, alpha):
    m, k = x.shape
    d = w.shape[2]
    tm = min(256, m)
    row = lambda i: (i, 0)
    fix = lambda i: (0, 0)
    return pl.pallas_call(
        functools.partial(_proj_ln_kernel, alpha=alpha),
        grid=(m // tm,),
        in_specs=[pl.BlockSpec((tm, k), row), pl.BlockSpec((None, k, d), lambda i: (layer, 0, 0)),
                  pl.BlockSpec((tm, d), row),
                  pl.BlockSpec((1, d), fix), pl.BlockSpec((1, d), fix)],
        out_specs=[pl.BlockSpec((tm, d), row), pl.BlockSpec((tm, d), row)],
        out_shape=[jax.ShapeDtypeStruct((m, d), F32), jax.ShapeDtypeStruct((m, d), BF16)],
        compiler_params=_cparams(("parallel",)),
        name="proj_residual_ln",
    )(x, w, h, g.reshape(1, d), b.reshape(1, d))


def _xattn_kernel(q_ref, k_ref, v_ref, wo_ref, h_ref, g_ref, b_ref, o_ref, op_ref, *, alpha):
    scale = XATTN_HEAD_DIM ** -0.5
    outs = []
    for hd in range(XATTN_HEADS):
        sl = slice(hd * XATTN_HEAD_DIM, (hd + 1) * XATTN_HEAD_DIM)
        s = _dot_nt(q_ref[:, sl], k_ref[:, sl]) * scale
        e = jnp.exp(s - jnp.max(s, axis=-1, keepdims=True))
        p = e / jnp.sum(e, axis=-1, keepdims=True)
        outs.append(_dot(p.astype(BF16), v_ref[:, sl]).astype(BF16))
    o = jnp.concatenate(outs, axis=1)
    y = _dot(o, wo_ref[...]) + alpha * h_ref[...]
    y = _ln_rows(y, g_ref[...], b_ref[...])
    o_ref[...] = y
    op_ref[...] = _pack_pairs(y[:, :HALF_D], y[:, HALF_D:])


def _xattn(q3, kv3, w_xo, layer, h3, g, b, alpha):
    bsz, s, d = q3.shape
    tm = min(256, s)
    row = lambda bi, i: (bi, i, 0)
    fix = lambda bi, i: (0, 0)
    return pl.pallas_call(
        functools.partial(_xattn_kernel, alpha=alpha),
        grid=(bsz, s // tm),
        in_specs=[
            pl.BlockSpec((None, tm, d), row),
            pl.BlockSpec((None, N_MEM, d), lambda bi, i: (bi, 0, 0)),
            pl.BlockSpec((None, N_MEM, d), lambda bi, i: (bi, 0, 1)),
            pl.BlockSpec((None, d, d), lambda bi, i: (layer, 0, 0)),
            pl.BlockSpec((None, tm, d), row),
            pl.BlockSpec((1, d), fix), pl.BlockSpec((1, d), fix),
        ],
        out_specs=[pl.BlockSpec((None, tm, d), row), pl.BlockSpec((None, tm, HALF_D), row)],
        out_shape=[jax.ShapeDtypeStruct((bsz, s, d), F32), jax.ShapeDtypeStruct((bsz, s, HALF_D), jnp.uint32)],
        compiler_params=_cparams(("parallel", "parallel")),
        name="memory_xattn",
    )(q3, kv3, kv3, w_xo, h3, g.reshape(1, d), b.reshape(1, d))


ROUTE_LANES = 128


def _router_kernel(h_ref, w_ref, b_ref, idx_ref, wt_ref, cnt_ref):
    lg = jnp.dot(h_ref[...], w_ref[...], preferred_element_type=F32,
                 precision=lax.Precision.HIGHEST) + b_ref[...]
    tm = lg.shape[0]
    lane = lax.broadcasted_iota(jnp.int32, (tm, N_EXPERTS), 1).astype(F32)
    vals, idxs, hits = [], [], []
    for _ in range(TOP_K):
        m = jnp.max(lg, axis=-1, keepdims=True)
        am = jnp.min(jnp.where(lg == m, lane, float(N_EXPERTS)), axis=-1, keepdims=True)
        vals.append(m)
        idxs.append(am.astype(jnp.int32))
        hits.append(lane == am)
        lg = jnp.where(hits[-1], -jnp.inf, lg)
    es = [jnp.exp(v - vals[0]) for v in vals]
    tot = es[0] + es[1] + es[2] + es[3]

    @pl.when(pl.program_id(0) == 0)
    def _():
        cnt_ref[...] = jnp.zeros_like(cnt_ref)

    tokhot = sum(h.astype(F32) for h in hits)
    ri = lax.broadcasted_iota(jnp.int32, (tm, tm), 0)
    ci = lax.broadcasted_iota(jnp.int32, (tm, tm), 1)
    before = _dot((ci < ri).astype(BF16), tokhot.astype(BF16)) + cnt_ref[...]
    ranks = [jnp.sum(jnp.where(h, before, 0.0), axis=-1, keepdims=True).astype(jnp.int32) for h in hits]
    cnt_ref[...] = cnt_ref[...] + jnp.sum(tokhot, axis=0, keepdims=True)

    out_lane = lax.broadcasted_iota(jnp.int32, (tm, ROUTE_LANES), 1)
    io = jnp.zeros((tm, ROUTE_LANES), jnp.int32)
    wo = jnp.zeros((tm, ROUTE_LANES), F32)
    for k in range(TOP_K):
        io = jnp.where(out_lane == k, idxs[k], io)
        io = jnp.where(out_lane == TOP_K + k, ranks[k], io)
        wo = jnp.where(out_lane == k, es[k] / tot, wo)
    idx_ref[...] = io
    wt_ref[...] = wo


def _router(h, w_router, b_router):
    m, d = h.shape
    tm = min(512, m)
    row = lambda i: (i, 0)
    fix = lambda i: (0, 0)
    idx, wt, cnt = pl.pallas_call(
        _router_kernel,
        grid=(m // tm,),
        in_specs=[pl.BlockSpec((tm, d), row), pl.BlockSpec((d, N_EXPERTS), fix),
                  pl.BlockSpec((1, N_EXPERTS), fix)],
        out_specs=[pl.BlockSpec((tm, ROUTE_LANES), row), pl.BlockSpec((tm, ROUTE_LANES), row),
                   pl.BlockSpec((1, N_EXPERTS), fix)],
        out_shape=[jax.ShapeDtypeStruct((m, ROUTE_LANES), jnp.int32),
                   jax.ShapeDtypeStruct((m, ROUTE_LANES), F32),
                   jax.ShapeDtypeStruct((1, N_EXPERTS), F32)],
        compiler_params=_cparams(("arbitrary",)),
        name="router_topk",
    )(h, w_router, b_router.reshape(1, N_EXPERTS))
    return idx[:, :TOP_K], idx[:, TOP_K:2 * TOP_K], wt, cnt.reshape(N_EXPERTS).astype(jnp.int32)


def _gather_rows(idx_ref, base, r0, n, src_hbm, dst, sem, priorities):
    for r in range(r0, r0 + n):
        row = idx_ref[base + r]
        pltpu.make_async_copy(src_hbm.at[pl.ds(row, 1), :], dst.at[pl.ds(r, 1), :], sem).start(
            priority=priorities[r % len(priorities)])


def _wait_rows(n, src_hbm, dst, sem):
    pltpu.make_async_copy(src_hbm.at[pl.ds(0, n), :], dst, sem).wait()


DISPATCH_TM = 256


def _dispatch_kernel(dest_ref, pend_ref, pad_ref, nu_ref, hp_ref, xs_hbm, zbuf, zsem, sem):
    i = pl.program_id(0)
    tm = hp_ref.shape[0]
    mb = MOE_BLOCK
    nblk = xs_hbm.shape[0] // mb

    @pl.when(i == 0)
    def _():
        zbuf[...] = jnp.zeros_like(zbuf)

        def zero_block(start):
            return pltpu.make_async_copy(zbuf, xs_hbm.at[pl.ds(pl.multiple_of(start, mb), mb), :], zsem)

        for e in range(N_EXPERTS):
            @pl.when(pad_ref[e] > 0)
            def _():
                zero_block(pend_ref[e] - mb).start()

        def tail_start(j, c):
            zero_block(j * mb).start()
            return c
        lax.fori_loop(nu_ref[0], nblk, tail_start, 0)
        for e in range(N_EXPERTS):
            @pl.when(pad_ref[e] > 0)
            def _():
                zero_block(0).wait()

        def tail_wait(j, c):
            zero_block(0).wait()
            return c
        lax.fori_loop(nu_ref[0], nblk, tail_wait, 0)

    base = i * (tm * TOP_K)
    for r in range(tm):
        for k in range(TOP_K):
            d = dest_ref[base + r * TOP_K + k]
            pltpu.make_async_copy(hp_ref.at[pl.ds(r, 1), :], xs_hbm.at[pl.ds(d, 1), :], sem).start(priority=k % 2)
    for k in range(TOP_K):
        pltpu.make_async_copy(hp_ref, xs_hbm.at[pl.ds(0, tm), :], sem).wait()


def _dispatch(hp, dest, pend, padded, n_used, n_slots):
    t = hp.shape[0]
    tm = min(DISPATCH_TM, t)
    grid_spec = pltpu.PrefetchScalarGridSpec(
        num_scalar_prefetch=4,
        grid=(t // tm,),
        in_specs=[pl.BlockSpec((tm, HALF_D), lambda i, ds, pe, pa, nu: (i, 0))],
        out_specs=pl.BlockSpec(memory_space=pl.ANY),
        scratch_shapes=[pltpu.VMEM((MOE_BLOCK, HALF_D), jnp.uint32), pltpu.SemaphoreType.DMA(()),
                        pltpu.SemaphoreType.DMA(())],
    )
    return pl.pallas_call(
        _dispatch_kernel,
        grid_spec=grid_spec,
        out_shape=jax.ShapeDtypeStruct((n_slots, HALF_D), jnp.uint32),
        compiler_params=_cparams(("arbitrary",)),
        name="moe_dispatch",
    )(dest.reshape(-1), pend, padded, n_used, hp)


def _moe_kernel(be_ref, nu_ref, xs_ref, wgu_ref, bgu_ref, wd_ref, bd_ref, o_ref):
    i = pl.program_id(0)
    nused = nu_ref[0]

    @pl.when(i < nused)
    def _():
        lo, hi = _unpack_pairs(xs_ref[...])
        x = jnp.concatenate([lo.astype(BF16), hi.astype(BF16)], axis=1)
        gu = _dot(x, wgu_ref[...]) + bgu_ref[...]
        gate = jnp.minimum(gu[:, :EXPERT_DIM], SWIGLU_LIMIT)
        up = jnp.clip(gu[:, EXPERT_DIM:], -SWIGLU_LIMIT, SWIGLU_LIMIT)
        act = (up + 1.0) * gate * _sigmoid(gate * SWIGLU_ALPHA)
        ye = _dot(act.astype(BF16), wd_ref[...]) + bd_ref[...]
        o_ref[...] = _pack_pairs(ye[:, :HALF_D], ye[:, HALF_D:])

    @pl.when(i >= nused)
    def _():
        o_ref[...] = jnp.zeros_like(o_ref)


def _moe_experts(xs, block_e, n_used, w_gu, b_gu, w_down, b_down, layer):
    d = D_MODEL
    nblk = block_e.shape[0]
    mb = MOE_BLOCK
    wsel = lambda i, be, nu: (layer, be[i], 0, 0)
    grid_spec = pltpu.PrefetchScalarGridSpec(
        num_scalar_prefetch=2,
        grid=(nblk,),
        in_specs=[
            pl.BlockSpec((mb, HALF_D), lambda i, be, nu: (jnp.minimum(i, jnp.maximum(nu[0] - 1, 0)), 0)),
            pl.BlockSpec((None, None, d, 2 * EXPERT_DIM), wsel),
            pl.BlockSpec((None, None, 1, 2 * EXPERT_DIM), wsel),
            pl.BlockSpec((None, None, EXPERT_DIM, d), wsel),
            pl.BlockSpec((None, None, 1, d), wsel),
        ],
        out_specs=pl.BlockSpec((mb, HALF_D), lambda i, be, nu: (i, 0)),
    )
    depth = w_gu.shape[0]
    return pl.pallas_call(
        _moe_kernel,
        grid_spec=grid_spec,
        out_shape=jax.ShapeDtypeStruct((nblk * mb, HALF_D), jnp.uint32),
        compiler_params=_cparams(("arbitrary",)),
        name="moe_experts",
    )(block_e, n_used, xs, w_gu, b_gu.reshape(depth, N_EXPERTS, 1, -1), w_down,
      b_down.reshape(depth, N_EXPERTS, 1, -1))


def _combine_kernel(dest_ref, yb_hbm, wt_ref, h_ref, g_ref, b_ref, o_ref, ob_ref, gbuf, sem, *, alpha, tm):
    i = pl.program_id(0)
    n = pl.num_programs(0)
    rows = TOP_K * tm
    prio = (0, 1)

    @pl.when(i == 0)
    def _():
        _gather_rows(dest_ref, 0, 0, rows, yb_hbm, gbuf.at[0], sem.at[0], prio)

    @pl.when(i + 1 < n)
    def _():
        nxt = (i + 1) % 2
        _gather_rows(dest_ref, (i + 1) * rows, 0, rows, yb_hbm, gbuf.at[nxt], sem.at[nxt], prio)

    slot = i % 2
    _wait_rows(rows, yb_hbm, gbuf.at[slot], sem.at[slot])
    wt = wt_ref[...]
    flo = fhi = None
    for k in range(TOP_K):
        lo, hi = _unpack_pairs(gbuf[slot, k * tm:(k + 1) * tm, :])
        wk = wt[:, k:k + 1]
        flo = wk * lo if flo is None else flo + wk * lo
        fhi = wk * hi if fhi is None else fhi + wk * hi
    ff = jnp.concatenate([flo, fhi], axis=1)
    y = _ln_rows(alpha * h_ref[...] + ff, g_ref[...], b_ref[...])
    o_ref[...] = y
    ob_ref[...] = y.astype(BF16)


def _moe_combine(yb, dest_blk, top_w, h, g, b, alpha, tm):
    t, d = h.shape
    row = lambda i, ds: (i, 0)
    fix = lambda i, ds: (0, 0)
    grid_spec = pltpu.PrefetchScalarGridSpec(
        num_scalar_prefetch=1,
        grid=(t // tm,),
        in_specs=[pl.BlockSpec(memory_space=pl.ANY), pl.BlockSpec((tm, ROUTE_LANES), row),
                  pl.BlockSpec((tm, d), row), pl.BlockSpec((1, d), fix), pl.BlockSpec((1, d), fix)],
        out_specs=[pl.BlockSpec((tm, d), row), pl.BlockSpec((tm, d), row)],
        scratch_shapes=[pltpu.VMEM((2, TOP_K * tm, HALF_D), jnp.uint32), pltpu.SemaphoreType.DMA((2,))],
    )
    return pl.pallas_call(
        functools.partial(_combine_kernel, alpha=alpha, tm=tm),
        grid_spec=grid_spec,
        out_shape=[jax.ShapeDtypeStruct((t, d), F32), jax.ShapeDtypeStruct((t, d), BF16)],
        compiler_params=_cparams(("arbitrary",)),
        name="moe_combine",
    )(dest_blk, yb, top_w, h, g.reshape(1, d), b.reshape(1, d))


def _moe_layer(h, hp, w_router, b_router, w_gu, b_gu, w_down, b_down, layer, g, b, alpha):
    t, d = h.shape
    top_idx, rank, top_w, counts = _router(h, w_router, b_router)
    n_assign = t * TOP_K
    padded = ((counts + MOE_BLOCK - 1) // MOE_BLOCK * MOE_BLOCK).astype(jnp.int32)
    pend = jnp.cumsum(padded).astype(jnp.int32)
    pstart = pend - padded
    dest = (pstart[top_idx] + rank).astype(jnp.int32)
    n_blocks = -(-n_assign // MOE_BLOCK) + N_EXPERTS
    n_slots = n_blocks * MOE_BLOCK
    blk_start = jnp.arange(n_blocks, dtype=jnp.int32) * MOE_BLOCK
    block_e = jnp.minimum(jnp.sum((pend[None, :] <= blk_start[:, None]).astype(jnp.int32), axis=1),
                          N_EXPERTS - 1).astype(jnp.int32)
    n_used = (pend[-1] // MOE_BLOCK).astype(jnp.int32).reshape(1)
    xs = _dispatch(hp, dest, pend, padded, n_used, n_slots)
    yb = _moe_experts(xs, block_e, n_used, w_gu, b_gu, w_down, b_down, layer)
    tm = min(128, t)
    dest_blk = dest.reshape(t // tm, tm, TOP_K).transpose(0, 2, 1).reshape(-1)
    return _moe_combine(yb, dest_blk, top_w, h, g, b, alpha, tm)


def kernel(x, mem, positions, ln_in_g, ln_in_b, ln_mem_g, ln_mem_b, w_in, conv_w, conv_b, dt_bias, a_log, d_skip, ssd_norm_g, w_ret_o, w_ssd_o, w_mix_o, ln1_g, ln1_b, w_xq, w_xkv, w_xo, ln2_g, ln2_b, w_router, b_router, w_gu, b_gu, w_down, b_down, ln3_g, ln3_b):
    bsz, s, d = x.shape
    t = bsz * s
    depth = w_in.shape[0]
    alpha = (2.0 * depth) ** 0.25
    h, hb = _layernorm(x.reshape(t, d), ln_in_g, ln_in_b)
    _, memb = _layernorm(mem.reshape(bsz * N_MEM, d), ln_mem_g, ln_mem_b)
    cos, sin = _rope_tables(positions)
    w_in_b, w_ret_b, w_ssd_b, w_mix_b = (w.astype(BF16) for w in (w_in, w_ret_o, w_ssd_o, w_mix_o))
    w_xq_b, w_xkv_b, w_xo_b = (w.astype(BF16) for w in (w_xq, w_xkv, w_xo))
    w_gu_b, w_down_b = w_gu.astype(BF16), w_down.astype(BF16)
    for l in range(depth):
        proj, dtraw = _in_proj(hb, w_in_b, l)
        proj3 = proj.reshape(bsz, s, MAIN_W)
        ret = _retention(proj3, cos, sin)
        xbc = _conv_silu(proj3, conv_w[l], conv_b[l])
        ssd = _ssd(xbc, proj3, dtraw.reshape(bsz, s, 2 * SSD_HEADS), dt_bias[l], a_log[l], d_skip[l],
                   ssd_norm_g[l])
        merged = _merge(ret.reshape(t, RET_V), ssd.reshape(t, SSD_D_INNER), proj, w_ret_b, w_ssd_b, l)
        h, hb = _proj_res_ln(merged, w_mix_b, l, h, ln1_g[l], ln1_b[l], alpha)

        q = _matmul(hb, w_xq_b, l, BF16, 1024, 1024, "xattn_q")
        kv = _matmul(memb, w_xkv_b, l, BF16, 512, 1024, "xattn_kv")
        h3, hp3 = _xattn(q.reshape(bsz, s, d), kv.reshape(bsz, N_MEM, 2 * d), w_xo_b, l,
                         h.reshape(bsz, s, d), ln2_g[l], ln2_b[l], alpha)
        h = h3.reshape(t, d)

        h, hb = _moe_layer(h, hp3.reshape(t, HALF_D), w_router[l], b_router[l], w_gu_b, b_gu, w_down_b, b_down, l,
                           ln3_g[l], ln3_b[l], alpha)
    return h.reshape(bsz, s, d)
```

```python
import functools

import jax
import jax.numpy as jnp
from jax import lax
from jax.experimental import pallas as pl
from jax.experimental.pallas import tpu as pltpu

F32 = jnp.float32
BF16 = jnp.bfloat16

D_MODEL = 2048
N_MEM = 256
RET_HEADS = 8
RET_QK_DIM = 128
RET_V_DIM = 256
RET_Q = RET_HEADS * RET_QK_DIM
RET_V = RET_HEADS * RET_V_DIM
RET_CHUNK = 128
ROPE_BASE = 10000.0
SSD_D_INNER = 2 * D_MODEL
SSD_HEAD_DIM = 64
SSD_HEADS = SSD_D_INNER // SSD_HEAD_DIM
SSD_GROUPS = 8
SSD_HPG = SSD_HEADS // SSD_GROUPS
SSD_GROUP_W = SSD_D_INNER // SSD_GROUPS
SSD_STATE = 128
SSD_BC = SSD_GROUPS * SSD_STATE
SSD_CONV = 5
SSD_SUB = 64
SSD_GROUPS_PER_STEP = 2
CONV_CH = SSD_D_INNER + 2 * SSD_BC
XATTN_HEADS = 4
XATTN_HEAD_DIM = D_MODEL // XATTN_HEADS
N_EXPERTS = 32
TOP_K = 4
EXPERT_DIM = D_MODEL // 2
SWIGLU_LIMIT = 7.0
SWIGLU_ALPHA = 1.702
MOE_BLOCK = 256
LN_EPS = 1e-5
NEG_BIG = -1e30

OFF_Q = 0
OFF_K = OFF_Q + RET_Q
OFF_V = OFF_K + RET_Q
OFF_GRET = OFF_V + RET_V
OFF_Z = OFF_GRET + RET_V
OFF_XS = OFF_Z + SSD_D_INNER
OFF_BM = OFF_XS + SSD_D_INNER
OFF_CM = OFF_BM + SSD_BC
OFF_DT = OFF_CM + SSD_BC
OFF_GATE_R = OFF_DT + 2 * SSD_HEADS
OFF_GATE_S = OFF_GATE_R + D_MODEL
IN_WIDTH = OFF_GATE_S + D_MODEL
MAIN_W = IN_WIDTH - 2 * SSD_HEADS
M_GATE_R = OFF_DT
M_GATE_S = OFF_DT + D_MODEL

VMEM_LIMIT_MB = 48


def _cparams(sem, vmem_mb=VMEM_LIMIT_MB):
    return pltpu.CompilerParams(dimension_semantics=sem, vmem_limit_bytes=vmem_mb * 1024 * 1024)


def _dot(a, b):
    return jnp.dot(a, b, preferred_element_type=F32)


def _dot_nt(a, b):
    return lax.dot_general(a, b, (((1,), (1,)), ((), ())), preferred_element_type=F32)


def _dot_tn(a, b):
    return lax.dot_general(a, b, (((0,), (0,)), ((), ())), preferred_element_type=F32)


def _sigmoid(x):
    return 1.0 / (1.0 + jnp.exp(-x))


def _softplus(x):
    return jnp.maximum(x, 0.0) + jnp.log(1.0 + jnp.exp(-jnp.abs(x)))


def _split2(a):
    a1 = a.astype(BF16)
    return a1, (a - a1.astype(F32)).astype(BF16)


HALF_D = D_MODEL // 2
HI_MASK = 0xFFFF0000


def _pack_pairs(lo, hi):
    lo_bits = lax.bitcast_convert_type(lo.astype(BF16).astype(F32), jnp.uint32)
    hi_bits = lax.bitcast_convert_type(hi.astype(BF16).astype(F32), jnp.uint32)
    return (lo_bits >> 16) | (hi_bits & jnp.uint32(HI_MASK))


def _unpack_pairs(u):
    lo = lax.bitcast_convert_type(u << 16, F32)
    hi = lax.bitcast_convert_type(u & jnp.uint32(HI_MASK), F32)
    return lo, hi


def _ln_rows(x, g, b):
    mu = jnp.mean(x, axis=-1, keepdims=True)
    xc = x - mu
    var = jnp.mean(xc * xc, axis=-1, keepdims=True)
    return xc * lax.rsqrt(var + LN_EPS) * g + b


def _ln_kernel(x_ref, g_ref, b_ref, o_ref, ob_ref):
    y = _ln_rows(x_ref[...], g_ref[...], b_ref[...])
    o_ref[...] = y
    ob_ref[...] = y.astype(BF16)


def _layernorm(x, g, b):
    m, d = x.shape
    tm = min(512, m)
    return pl.pallas_call(
        _ln_kernel,
        grid=(m // tm,),
        in_specs=[pl.BlockSpec((tm, d), lambda i: (i, 0)),
                  pl.BlockSpec((1, d), lambda i: (0, 0)),
                  pl.BlockSpec((1, d), lambda i: (0, 0))],
        out_specs=[pl.BlockSpec((tm, d), lambda i: (i, 0)),
                   pl.BlockSpec((tm, d), lambda i: (i, 0))],
        out_shape=[jax.ShapeDtypeStruct((m, d), F32), jax.ShapeDtypeStruct((m, d), BF16)],
        compiler_params=_cparams(("parallel",)),
        name="layernorm",
    )(x, g.reshape(1, d), b.reshape(1, d))


def _mm_kernel(x_ref, w_ref, o_ref):
    o_ref[...] = _dot(x_ref[...], w_ref[...]).astype(o_ref.dtype)


def _matmul(x, w, layer, out_dtype, tm, tn, name):
    m, k = x.shape
    n = w.shape[2]
    tm = min(tm, m)
    tn = min(tn, n)
    return pl.pallas_call(
        _mm_kernel,
        grid=(m // tm, n // tn),
        in_specs=[pl.BlockSpec((tm, k), lambda i, j: (i, 0)),
                  pl.BlockSpec((None, k, tn), lambda i, j: (layer, 0, j))],
        out_specs=pl.BlockSpec((tm, tn), lambda i, j: (i, j)),
        out_shape=jax.ShapeDtypeStruct((m, n), out_dtype),
        compiler_params=_cparams(("parallel", "arbitrary")),
        name=name,
    )(x, w)


def _mm_lead_kernel(x_ref, w_ref, o_ref):
    o_ref[...] = _dot(x_ref[...], w_ref[0]).astype(o_ref.dtype)


def _in_proj(hb, w, layer):
    m, k = hb.shape
    tm = min(1024, m)
    tn = 1024
    nmain = OFF_DT // tn
    dtw = 2 * SSD_HEADS
    proj = pl.pallas_call(
        _mm_lead_kernel,
        grid=(m // tm, MAIN_W // tn),
        in_specs=[pl.BlockSpec((tm, k), lambda i, j: (i, 0)),
                  pl.BlockSpec((pl.Element(1), pl.Element(k), pl.Element(tn)),
                               lambda i, j: (layer, 0,
                                             pl.multiple_of(jnp.where(j < nmain, j * tn, j * tn + dtw), dtw)))],
        out_specs=pl.BlockSpec((tm, tn), lambda i, j: (i, j)),
        out_shape=jax.ShapeDtypeStruct((m, MAIN_W), BF16),
        compiler_params=_cparams(("parallel", "arbitrary")),
        name="in_proj",
    )(hb, w)
    dtraw = pl.pallas_call(
        _mm_kernel,
        grid=(m // tm,),
        in_specs=[pl.BlockSpec((tm, k), lambda i: (i, 0)),
                  pl.BlockSpec((None, k, dtw), lambda i: (layer, 0, OFF_DT // dtw))],
        out_specs=pl.BlockSpec((tm, dtw), lambda i: (i, 0)),
        out_shape=jax.ShapeDtypeStruct((m, dtw), F32),
        compiler_params=_cparams(("parallel",)),
        name="dt_proj",
    )(hb, w)
    return proj, dtraw


def _rope_kernel(pos_ref, inv_ref, sgn_ref, cos_ref, sin_ref):
    ang = pos_ref[...] * inv_ref[...]
    cos_ref[...] = jnp.cos(ang)
    sin_ref[...] = jnp.sin(ang) * sgn_ref[...]


def _rope_tables(positions):
    t = positions.size
    half = RET_QK_DIM // 2
    inv = ROPE_BASE ** (-jnp.arange(half, dtype=F32) / half)
    inv2 = jnp.concatenate([inv, inv]).reshape(1, RET_QK_DIM)
    sgn = jnp.concatenate([-jnp.ones((half,), F32), jnp.ones((half,), F32)]).reshape(1, RET_QK_DIM)
    pos = positions.astype(F32).reshape(t, 1)
    tm = min(1024, t)
    return pl.pallas_call(
        _rope_kernel,
        grid=(t // tm,),
        in_specs=[pl.BlockSpec((tm, 1), lambda i: (i, 0)),
                  pl.BlockSpec((1, RET_QK_DIM), lambda i: (0, 0)),
                  pl.BlockSpec((1, RET_QK_DIM), lambda i: (0, 0))],
        out_specs=[pl.BlockSpec((tm, RET_QK_DIM), lambda i: (i, 0)),
                   pl.BlockSpec((tm, RET_QK_DIM), lambda i: (i, 0))],
        out_shape=[jax.ShapeDtypeStruct((t, RET_QK_DIM), F32)] * 2,
        compiler_params=_cparams(("parallel",)),
        name="rope_tables",
    )(pos, inv2, sgn)


def _ret_kernel(*refs, reverse, nchunk):
    if not reverse:
        (q_ref, k_ref, v_ref, cos_ref, sin_ref, qd_ref, kd_ref, cd_ref, dm_ref, y_ref, st_ref) = refs
    else:
        (q_ref, k_ref, v_ref, cos_ref, sin_ref, qd_ref, kd_ref, cd_ref, yf_ref, g_ref, o_ref, st_ref) = refs

    @pl.when(pl.program_id(2) == 0)
    def _():
        st_ref[...] = jnp.zeros_like(st_ref)

    qd = qd_ref[...]
    kd = kd_ref[...]
    cd = cd_ref[...]
    scale = RET_QK_DIM ** -0.5
    c = RET_CHUNK
    order = range(nchunk - 1, -1, -1) if reverse else range(nchunk)
    st = st_ref[...]
    for ci in order:
        sl = slice(ci * c, (ci + 1) * c)
        cs = cos_ref[sl, :]
        sn = sin_ref[sl, :]
        q = q_ref[sl, :].astype(F32)
        k = k_ref[sl, :].astype(F32)
        v = v_ref[sl, :]
        qr = q * cs + pltpu.roll(q, RET_QK_DIM // 2, 1) * sn
        kr = (k * cs + pltpu.roll(k, RET_QK_DIM // 2, 1) * sn) * scale
        y = _dot((qr * qd).astype(BF16), st.astype(BF16))
        st = cd * st + _dot_tn((kr * kd).astype(BF16), v)
        if not reverse:
            s = _dot_nt(qr.astype(BF16), kr.astype(BF16)) * dm_ref[...]
            y_ref[sl, :] = y + _dot(s.astype(BF16), v)
        else:
            tot = yf_ref[sl, :] + y
            mu = jnp.mean(tot, axis=-1, keepdims=True)
            tc = tot - mu
            var = jnp.mean(tc * tc, axis=-1, keepdims=True)
            g = g_ref[sl, :].astype(F32)
            o_ref[sl, :] = (tc * lax.rsqrt(var + LN_EPS) * (g * _sigmoid(g))).astype(BF16)
    st_ref[...] = st


def _ret_decay_tables():
    lg = jnp.log1p(-jnp.exp2(-5.0 - jnp.arange(RET_HEADS, dtype=F32)))
    idx = jnp.arange(RET_CHUNK, dtype=F32)
    c = float(RET_CHUNK)
    dist = jnp.abs(idx[:, None] - idx[None, :])
    dmat = jnp.exp(lg[:, None, None] * dist)

    def rows(e):
        return jnp.broadcast_to(jnp.exp(lg[:, None] * e)[..., None], (RET_HEADS, RET_CHUNK, RET_QK_DIM))

    qd_f = rows(idx + 1.0)
    kd_f = rows(c - 1.0 - idx)
    qd_b = rows(c - idx)
    kd_b = rows(idx)
    cd = jnp.broadcast_to(jnp.exp(lg * c)[:, None, None], (RET_HEADS, 1, RET_V_DIM))
    return dmat, (qd_f, kd_f), (qd_b, kd_b), cd


def _retention(proj3, cos, sin):
    b, s, _ = proj3.shape
    rb = min(2048, s)
    nb = s // rb
    nchunk = rb // RET_CHUNK
    dmat, dec_f, dec_b, cd = _ret_decay_tables()
    cos3 = cos.reshape(b, s, RET_QK_DIM)
    sin3 = sin.reshape(b, s, RET_QK_DIM)
    kq = OFF_K // RET_QK_DIM
    kv = OFF_V // RET_V_DIM
    kg = OFF_GRET // RET_V_DIM
    tab = lambda w: pl.BlockSpec((None, RET_CHUNK, w), lambda bi, h, i: (h, 0, 0))
    cd_spec = pl.BlockSpec((None, 1, RET_V_DIM), lambda bi, h, i: (h, 0, 0))

    def specs(rev):
        blk = (lambda i: nb - 1 - i) if rev else (lambda i: i)
        return [
            pl.BlockSpec((None, rb, RET_QK_DIM), lambda bi, h, i: (bi, blk(i), h)),
            pl.BlockSpec((None, rb, RET_QK_DIM), lambda bi, h, i: (bi, blk(i), kq + h)),
            pl.BlockSpec((None, rb, RET_V_DIM), lambda bi, h, i: (bi, blk(i), kv + h)),
            pl.BlockSpec((None, rb, RET_QK_DIM), lambda bi, h, i: (bi, blk(i), 0)),
            pl.BlockSpec((None, rb, RET_QK_DIM), lambda bi, h, i: (bi, blk(i), 0)),
            tab(RET_QK_DIM), tab(RET_QK_DIM), cd_spec,
        ], blk

    sem = ("parallel", "parallel", "arbitrary")
    scratch = [pltpu.VMEM((RET_QK_DIM, RET_V_DIM), F32)]
    in_f, _ = specs(False)
    y_f = pl.pallas_call(
        functools.partial(_ret_kernel, reverse=False, nchunk=nchunk),
        grid=(b, RET_HEADS, nb),
        in_specs=in_f + [tab(RET_CHUNK)],
        out_specs=pl.BlockSpec((None, rb, RET_V_DIM), lambda bi, h, i: (bi, i, h)),
        out_shape=jax.ShapeDtypeStruct((b, s, RET_V), F32),
        scratch_shapes=scratch,
        compiler_params=_cparams(sem),
        name="retention_fwd",
    )(proj3, proj3, proj3, cos3, sin3, dec_f[0], dec_f[1], cd, dmat)
    in_b, blk = specs(True)
    out = pl.pallas_call(
        functools.partial(_ret_kernel, reverse=True, nchunk=nchunk),
        grid=(b, RET_HEADS, nb),
        in_specs=in_b + [
            pl.BlockSpec((None, rb, RET_V_DIM), lambda bi, h, i: (bi, blk(i), h)),
            pl.BlockSpec((None, rb, RET_V_DIM), lambda bi, h, i: (bi, blk(i), kg + h)),
        ],
        out_specs=pl.BlockSpec((None, rb, RET_V_DIM), lambda bi, h, i: (bi, blk(i), h)),
        out_shape=jax.ShapeDtypeStruct((b, s, RET_V), BF16),
        scratch_shapes=scratch,
        compiler_params=_cparams(sem),
        name="retention_bwd",
    )(proj3, proj3, proj3, cos3, sin3, dec_b[0], dec_b[1], cd, y_f, proj3)
    return out


CONV_HALO = 16


def _conv_kernel(xm_ref, xp_ref, xn_ref, w_ref, b_ref, o_ref):
    i = pl.program_id(1)
    n = pl.num_programs(1)
    tm = xm_ref.shape[0]
    pad = SSD_CONV // 2
    ext = jnp.concatenate([jnp.where(i == 0, 0.0, xp_ref[...].astype(F32)),
                           xm_ref[...].astype(F32),
                           jnp.where(i == n - 1, 0.0, xn_ref[...].astype(F32))], axis=0)
    rows = tm + 2 * CONV_HALO
    acc = b_ref[...] + w_ref[pad:pad + 1, :] * ext[CONV_HALO:CONV_HALO + tm, :]
    for k in range(SSD_CONV):
        if k != pad:
            sh = pltpu.roll(ext, (pad - k) % rows, 0)
            acc = acc + w_ref[k:k + 1, :] * sh[CONV_HALO:CONV_HALO + tm, :]
    o_ref[...] = (acc * _sigmoid(acc)).astype(BF16)


def _conv_silu(proj3, conv_w, conv_b):
    b, s, _ = proj3.shape
    tm = min(512, s)
    tc = 512
    ns = s // tm
    c0 = OFF_XS // tc
    hb = tm // CONV_HALO
    last = s // CONV_HALO - 1
    return pl.pallas_call(
        _conv_kernel,
        grid=(b, ns, CONV_CH // tc),
        in_specs=[
            pl.BlockSpec((None, tm, tc), lambda bi, i, j: (bi, i, c0 + j)),
            pl.BlockSpec((None, CONV_HALO, tc), lambda bi, i, j: (bi, jnp.maximum(i * hb - 1, 0), c0 + j)),
            pl.BlockSpec((None, CONV_HALO, tc), lambda bi, i, j: (bi, jnp.minimum((i + 1) * hb, last), c0 + j)),
            pl.BlockSpec((SSD_CONV, tc), lambda bi, i, j: (0, j)),
            pl.BlockSpec((1, tc), lambda bi, i, j: (0, j)),
        ],
        out_specs=pl.BlockSpec((None, tm, tc), lambda bi, i, j: (bi, i, j)),
        out_shape=jax.ShapeDtypeStruct((b, s, CONV_CH), BF16),
        compiler_params=_cparams(("parallel", "parallel", "parallel")),
        name="conv_silu",
    )(proj3, proj3, proj3, conv_w, conv_b.reshape(1, CONV_CH))


class _ColumnView:
    def __init__(self, ref, k, w):
        self.ref, self.lo, self.hi = ref, k * w, (k + 1) * w

    def _rows(self, idx):
        return slice(None) if idx is Ellipsis else idx[0]

    def __getitem__(self, idx):
        return self.ref[self._rows(idx), self.lo:self.hi]

    def __setitem__(self, idx, value):
        self.ref[self._rows(idx), self.lo:self.hi] = value


def _ssd_kernel(*refs, reverse, nchunk, ngrp):
    n = ngrp
    gw = SSD_GROUP_W
    cols = lambda ref, w: [_ColumnView(ref, k, w) for k in range(n)]
    x_refs, b_refs, c_refs = cols(refs[0], gw), cols(refs[1], SSD_STATE), cols(refs[2], SSD_STATE)
    dt_ref, bias_ref, alog_ref = refs[3:6]
    if not reverse:
        y_refs, h_refs = cols(refs[6], gw), refs[7:7 + n]
    else:
        yf_refs, z_refs, dsk_refs, ng_refs, o_refs = (cols(r, gw) for r in refs[6:11])
        h_refs = refs[11:11 + n]

    @pl.when(pl.program_id(2) == 0)
    def _():
        for h_ref in h_refs:
            h_ref[...] = jnp.zeros_like(h_ref)

    sub = SSD_SUB
    ch = 2 * sub
    i32 = jnp.int32
    ci = lax.broadcasted_iota(i32, (ch, gw), 0)
    li = lax.broadcasted_iota(i32, (ch, gw), 1)
    rn = lax.broadcasted_iota(i32, (16, ch), 0)
    cn = lax.broadcasted_iota(i32, (16, ch), 1)
    selws, selns = [], []
    for k in range(n):
        g = pl.program_id(1) * n + k
        base = (SSD_HEADS if reverse else 0) + g * SSD_HPG
        selws.append((ci == base + (li >> 6)).astype(BF16))
        selns.append(jnp.logical_and(cn == base + rn, rn < SSD_HPG).astype(BF16))
    ii = lax.broadcasted_iota(i32, (ch, ch), 0)
    jj = lax.broadcasted_iota(i32, (ch, ch), 1)
    tri = ((jj >= ii) if reverse else (jj <= ii)).astype(BF16)
    jh = jj & (sub - 1)
    bmask = (ii >> 6) == (jj >> 6)
    if reverse:
        mask_full = (jh + sub) > ii
        mask_half = (jh > ii)[:sub]
    else:
        mask_full = ii >= jh
        mask_half = (ii >= jh)[:sub]
    bias = bias_ref[...]
    a_neg = -jnp.exp(alog_ref[...])

    hsts = [h_ref[...] for h_ref in h_refs]

    def _ssd_group_chunk(k, rows, pieces, ps):
        wide = _dot(pieces, selws[k])
        dtw = wide[:ch] + wide[ch:2 * ch]
        pw = wide[2 * ch:3 * ch] + wide[3 * ch:]
        pn = sum(_dot_nt(selns[k], s) for s in ps)
        x = x_refs[k][rows, :].astype(F32)
        xdt = x * dtw
        bm = b_refs[k][rows, :]
        cm = c_refs[k][rows, :]
        bm_a = jnp.concatenate([bm[:sub], bm[:sub]], axis=0)
        bm_b = jnp.concatenate([bm[sub:], bm[sub:]], axis=0)
        if reverse:
            cb_full = _dot_nt(cm, bm_b)
            cb_half = _dot_nt(cm[:sub], bm_a)
        else:
            cb_full = _dot_nt(cm, bm_a)
            cb_half = _dot_nt(cm[sub:], bm_b)
        parts = []
        for p in range(SSD_HPG // 2):
            colp = pw[:, p * ch:(p + 1) * ch]
            row_a = jnp.concatenate([pn[2 * p:2 * p + 1, :sub], pn[2 * p + 1:2 * p + 2, :sub]], axis=1)
            row_b = jnp.concatenate([pn[2 * p:2 * p + 1, sub:], pn[2 * p + 1:2 * p + 2, sub:]], axis=1)
            xp = xdt[:, p * ch:(p + 1) * ch].astype(BF16)
            xb_a = jnp.where(bmask, jnp.concatenate([xp[:sub], xp[:sub]], axis=0), 0)
            xb_b = jnp.where(bmask, jnp.concatenate([xp[sub:], xp[sub:]], axis=0), 0)
            if reverse:
                w_full = (cb_full * jnp.exp(jnp.where(mask_full, colp - row_b, NEG_BIG))).astype(BF16)
                w_half = (cb_half * jnp.exp(jnp.where(mask_half, colp[:sub] - row_a, NEG_BIG))).astype(BF16)
                y_full = _dot(w_full, xb_b)
                y_half = _dot(w_half, xb_a)
                parts.append(jnp.concatenate([y_full[:sub] + y_half, y_full[sub:]], axis=0))
            else:
                w_full = (cb_full * jnp.exp(jnp.where(mask_full, colp - row_a, NEG_BIG))).astype(BF16)
                w_half = (cb_half * jnp.exp(jnp.where(mask_half, colp[sub:] - row_b, NEG_BIG))).astype(BF16)
                y_full = _dot(w_full, xb_a)
                y_half = _dot(w_half, xb_b)
                parts.append(jnp.concatenate([y_full[:sub], y_full[sub:] + y_half], axis=0))
        y = jnp.concatenate(parts, axis=1) + _dot(cm, hsts[k].astype(BF16)) * jnp.exp(pw)
        plast = pw[0:1, :] if reverse else pw[ch - 1:ch, :]
        xdec = (jnp.exp(plast - pw) * xdt).astype(BF16)
        hsts[k] = jnp.exp(plast) * hsts[k] + _dot_tn(bm, xdec)
        if not reverse:
            y_refs[k][rows, :] = y
        else:
            tot = yf_refs[k][rows, :] + y + dsk_refs[k][...] * x
            z = z_refs[k][rows, :].astype(F32)
            tot = tot * (z * _sigmoid(z))
            ms = jnp.mean(tot * tot, axis=-1, keepdims=True)
            o_refs[k][rows, :] = (tot * lax.rsqrt(ms + LN_EPS) * ng_refs[k][...]).astype(BF16)

    for t in range(nchunk):
        sc = (nchunk - 1 - t) if reverse else t
        rows = slice(sc * ch, (sc + 1) * ch)
        dt_all = _softplus(dt_ref[rows, :] + bias)
        p_all = sum(_dot(tri, s) for s in _split2(dt_all * a_neg))
        ps = _split2(p_all)
        pieces = jnp.concatenate(_split2(dt_all) + ps, axis=0)
        for k in range(n):
            _ssd_group_chunk(k, rows, pieces, ps)
    for h_ref, hst in zip(h_refs, hsts):
        h_ref[...] = hst


def _ssd(xbc, proj3, dtraw3, dt_bias, a_log, d_skip, norm_g):
    b, s, _ = xbc.shape
    rb = min(1024, s)
    nb = s // rb
    nchunk = rb // (2 * SSD_SUB)
    ngrp = SSD_GROUPS_PER_STEP
    gw = ngrp * SSD_GROUP_W
    sw = ngrp * SSD_STATE
    bias = dt_bias.astype(F32).reshape(1, 2 * SSD_HEADS)
    alog = a_log.astype(F32).reshape(1, 2 * SSD_HEADS)
    dsk = jnp.broadcast_to(d_skip.astype(F32)[:, None], (SSD_HEADS, SSD_HEAD_DIM)).reshape(SSD_GROUPS // ngrp, 1, gw)
    ng = norm_g.astype(F32).reshape(SSD_GROUPS // ngrp, 1, gw)
    kb = SSD_D_INNER // sw
    kc = (SSD_D_INNER + SSD_BC) // sw
    kz = OFF_Z // gw

    def specs(rev):
        blk = (lambda i: nb - 1 - i) if rev else (lambda i: i)
        small = pl.BlockSpec((1, 2 * SSD_HEADS), lambda bi, g, i: (0, 0))
        return [
            pl.BlockSpec((None, rb, gw), lambda bi, g, i: (bi, blk(i), g)),
            pl.BlockSpec((None, rb, sw), lambda bi, g, i: (bi, blk(i), kb + g)),
            pl.BlockSpec((None, rb, sw), lambda bi, g, i: (bi, blk(i), kc + g)),
            pl.BlockSpec((None, rb, 2 * SSD_HEADS), lambda bi, g, i: (bi, blk(i), 0)),
            small, small,
        ], blk

    sem = ("parallel", "parallel", "arbitrary")
    scratch = [pltpu.VMEM((SSD_STATE, SSD_GROUP_W), F32) for _ in range(ngrp)]
    in_f, _ = specs(False)
    y_f = pl.pallas_call(
        functools.partial(_ssd_kernel, reverse=False, nchunk=nchunk, ngrp=ngrp),
        grid=(b, SSD_GROUPS // ngrp, nb),
        in_specs=in_f,
        out_specs=pl.BlockSpec((None, rb, gw), lambda bi, g, i: (bi, i, g)),
        out_shape=jax.ShapeDtypeStruct((b, s, SSD_D_INNER), F32),
        scratch_shapes=scratch,
        compiler_params=_cparams(sem),
        name="ssd_fwd",
    )(xbc, xbc, xbc, dtraw3, bias, alog)
    in_b, blk = specs(True)
    grp = pl.BlockSpec((None, 1, gw), lambda bi, g, i: (g, 0, 0))
    out = pl.pallas_call(
        functools.partial(_ssd_kernel, reverse=True, nchunk=nchunk, ngrp=ngrp),
        grid=(b, SSD_GROUPS // ngrp, nb),
        in_specs=in_b + [
            pl.BlockSpec((None, rb, gw), lambda bi, g, i: (bi, blk(i), g)),
            pl.BlockSpec((None, rb, gw), lambda bi, g, i: (bi, blk(i), kz + g)),
            grp, grp,
        ],
        out_specs=pl.BlockSpec((None, rb, gw), lambda bi, g, i: (bi, blk(i), g)),
        out_shape=jax.ShapeDtypeStruct((b, s, SSD_D_INNER), BF16),
        scratch_shapes=scratch,
        compiler_params=_cparams(sem),
        name="ssd_bwd",
    )(xbc, xbc, xbc, dtraw3, bias, alog, y_f, proj3, dsk, ng)
    return out


def _merge_kernel(ret_ref, ssd_ref, wr_ref, ws_ref, gr_ref, gs_ref, o_ref):
    yr = _dot(ret_ref[...], wr_ref[...])
    ys = _dot(ssd_ref[...], ws_ref[...])
    o = _sigmoid(gr_ref[...].astype(F32)) * yr + _sigmoid(gs_ref[...].astype(F32)) * ys
    o_ref[...] = o.astype(BF16)


def _merge(ret, ssd, proj, w_ret_o, w_ssd_o, layer):
    m = ret.shape[0]
    tm = min(1024, m)
    tn = 256
    gr0 = M_GATE_R // tn
    gs0 = M_GATE_S // tn
    return pl.pallas_call(
        _merge_kernel,
        grid=(m // tm, D_MODEL // tn),
        in_specs=[
            pl.BlockSpec((tm, RET_V), lambda i, j: (i, 0)),
            pl.BlockSpec((tm, SSD_D_INNER), lambda i, j: (i, 0)),
            pl.BlockSpec((None, RET_V, tn), lambda i, j: (layer, 0, j)),
            pl.BlockSpec((None, SSD_D_INNER, tn), lambda i, j: (layer, 0, j)),
            pl.BlockSpec((tm, tn), lambda i, j: (i, gr0 + j)),
            pl.BlockSpec((tm, tn), lambda i, j: (i, gs0 + j)),
        ],
        out_specs=pl.BlockSpec((tm, tn), lambda i, j: (i, j)),
        out_shape=jax.ShapeDtypeStruct((m, D_MODEL), BF16),
        compiler_params=_cparams(("parallel", "arbitrary")),
        name="branch_merge",
    )(ret, ssd, w_ret_o, w_ssd_o, proj, proj)


def _proj_ln_kernel(x_ref, w_ref, h_ref, g_ref, b_ref, o_ref, ob_ref, *, alpha):
    y = _dot(x_ref[...], w_ref[...]) + alpha * h_ref[...]
    y = _ln_rows(y, g_ref[...], b_ref[...])
    o_ref[...] = y
    ob_ref[...] = y.astype(BF16)


def _proj_res_ln(x, w, layer, h, g, b, alpha):
    m, k = x.shape
    d = w.shape[2]
    tm = min(256, m)
    row = lambda i: (i, 0)
    fix = lambda i: (0, 0)
    return pl.pallas_call(
        functools.partial(_proj_ln_kernel, alpha=alpha),
        grid=(m // tm,),
        in_specs=[pl.BlockSpec((tm, k), row), pl.BlockSpec((None, k, d), lambda i: (layer, 0, 0)),
                  pl.BlockSpec((tm, d), row),
                  pl.BlockSpec((1, d), fix), pl.BlockSpec((1, d), fix)],
        out_specs=[pl.BlockSpec((tm, d), row), pl.BlockSpec((tm, d), row)],
        out_shape=[jax.ShapeDtypeStruct((m, d), F32), jax.ShapeDtypeStruct((m, d), BF16)],
        compiler_params=_cparams(("parallel",)),
        name="proj_residual_ln",
    )(x, w, h, g.reshape(1, d), b.reshape(1, d))


def _xattn_kernel(q_ref, k_ref, v_ref, wo_ref, h_ref, g_ref, b_ref, o_ref, op_ref, *, alpha):
    scale = XATTN_HEAD_DIM ** -0.5
    outs = []
    for hd in range(XATTN_HEADS):
        sl = slice(hd * XATTN_HEAD_DIM, (hd + 1) * XATTN_HEAD_DIM)
        s = _dot_nt(q_ref[:, sl], k_ref[:, sl]) * scale
        e = jnp.exp(s - jnp.max(s, axis=-1, keepdims=True))
        p = e / jnp.sum(e, axis=-1, keepdims=True)
        outs.append(_dot(p.astype(BF16), v_ref[:, sl]).astype(BF16))
    o = jnp.concatenate(outs, axis=1)
    y = _dot(o, wo_ref[...]) + alpha * h_ref[...]
    y = _ln_rows(y, g_ref[...], b_ref[...])
    o_ref[...] = y
    op_ref[...] = _pack_pairs(y[:, :HALF_D], y[:, HALF_D:])


def _xattn(q3, kv3, w_xo, layer, h3, g, b, alpha):
    bsz, s, d = q3.shape
    tm = min(256, s)
    row = lambda bi, i: (bi, i, 0)
    fix = lambda bi, i: (0, 0)
    return pl.pallas_call(
        functools.partial(_xattn_kernel, alpha=alpha),
        grid=(bsz, s // tm),
        in_specs=[
            pl.BlockSpec((None, tm, d), row),
            pl.BlockSpec((None, N_MEM, d), lambda bi, i: (bi, 0, 0)),
            pl.BlockSpec((None, N_MEM, d), lambda bi, i: (bi, 0, 1)),
            pl.BlockSpec((None, d, d), lambda bi, i: (layer, 0, 0)),
            pl.BlockSpec((None, tm, d), row),
            pl.BlockSpec((1, d), fix), pl.BlockSpec((1, d), fix),
        ],
        out_specs=[pl.BlockSpec((None, tm, d), row), pl.BlockSpec((None, tm, HALF_D), row)],
        out_shape=[jax.ShapeDtypeStruct((bsz, s, d), F32), jax.ShapeDtypeStruct((bsz, s, HALF_D), jnp.uint32)],
        compiler_params=_cparams(("parallel", "parallel")),
        name="memory_xattn",
    )(q3, kv3, kv3, w_xo, h3, g.reshape(1, d), b.reshape(1, d))


ROUTE_LANES = 128


def _router_kernel(h_ref, w_ref, b_ref, idx_ref, wt_ref, cnt_ref):
    h1, h2 = _split2(h_ref[...])
    w1, w2 = _split2(w_ref[...])
    lg = _dot(h1, w1) + (_dot(h1, w2) + _dot(h2, w1)) + b_ref[...]
    tm = lg.shape[0]
    lane = lax.broadcasted_iota(jnp.int32, (tm, N_EXPERTS), 1).astype(F32)
    vals, idxs, hits = [], [], []
    for _ in range(TOP_K):
        m = jnp.max(lg, axis=-1, keepdims=True)
        am = jnp.min(jnp.where(lg == m, lane, float(N_EXPERTS)), axis=-1, keepdims=True)
        vals.append(m)
        idxs.append(am.astype(jnp.int32))
        hits.append(lane == am)
        lg = jnp.where(hits[-1], -jnp.inf, lg)
    es = [jnp.exp(v - vals[0]) for v in vals]
    tot = es[0] + es[1] + es[2] + es[3]

    @pl.when(pl.program_id(0) == 0)
    def _():
        cnt_ref[...] = jnp.zeros_like(cnt_ref)

    tokhot = sum(h.astype(F32) for h in hits)
    ri = lax.broadcasted_iota(jnp.int32, (tm, tm), 0)
    ci = lax.broadcasted_iota(jnp.int32, (tm, tm), 1)
    before = _dot((ci < ri).astype(BF16), tokhot.astype(BF16)) + cnt_ref[...]
    ranks = [jnp.sum(jnp.where(h, before, 0.0), axis=-1, keepdims=True).astype(jnp.int32) for h in hits]
    cnt_ref[...] = cnt_ref[...] + jnp.sum(tokhot, axis=0, keepdims=True)

    out_lane = lax.broadcasted_iota(jnp.int32, (tm, ROUTE_LANES), 1)
    io = jnp.zeros((tm, ROUTE_LANES), jnp.int32)
    wo = jnp.zeros((tm, ROUTE_LANES), F32)
    for k in range(TOP_K):
        io = jnp.where(out_lane == k, idxs[k], io)
        io = jnp.where(out_lane == TOP_K + k, ranks[k], io)
        wo = jnp.where(out_lane == k, es[k] / tot, wo)
    idx_ref[...] = io
    wt_ref[...] = wo


def _router(h, w_router, b_router):
    m, d = h.shape
    tm = min(512, m)
    row = lambda i: (i, 0)
    fix = lambda i: (0, 0)
    idx, wt, cnt = pl.pallas_call(
        _router_kernel,
        grid=(m // tm,),
        in_specs=[pl.BlockSpec((tm, d), row), pl.BlockSpec((d, N_EXPERTS), fix),
                  pl.BlockSpec((1, N_EXPERTS), fix)],
        out_specs=[pl.BlockSpec((tm, ROUTE_LANES), row), pl.BlockSpec((tm, ROUTE_LANES), row),
                   pl.BlockSpec((1, N_EXPERTS), fix)],
        out_shape=[jax.ShapeDtypeStruct((m, ROUTE_LANES), jnp.int32),
                   jax.ShapeDtypeStruct((m, ROUTE_LANES), F32),
                   jax.ShapeDtypeStruct((1, N_EXPERTS), F32)],
        compiler_params=_cparams(("arbitrary",)),
        name="router_topk",
    )(h, w_router, b_router.reshape(1, N_EXPERTS))
    return idx[:, :TOP_K], idx[:, TOP_K:2 * TOP_K], wt, cnt.reshape(N_EXPERTS).astype(jnp.int32)


def _gather_rows(idx_ref, base, r0, n, src_hbm, dst, sem, priorities):
    for r in range(r0, r0 + n):
        row = idx_ref[base + r]
        pltpu.make_async_copy(src_hbm.at[pl.ds(row, 1), :], dst.at[pl.ds(r, 1), :], sem).start(
            priority=priorities[r % len(priorities)])


def _wait_rows(n, src_hbm, dst, sem):
    pltpu.make_async_copy(src_hbm.at[pl.ds(0, n), :], dst, sem).wait()


DISPATCH_TM = 256


def _dispatch_kernel(dest_ref, pend_ref, pad_ref, nu_ref, hp_ref, xs_hbm, zbuf, zsem, sem):
    i = pl.program_id(0)
    tm = hp_ref.shape[0]
    mb = MOE_BLOCK
    nblk = xs_hbm.shape[0] // mb

    @pl.when(i == 0)
    def _():
        zbuf[...] = jnp.zeros_like(zbuf)

        def zero_block(start):
            return pltpu.make_async_copy(zbuf, xs_hbm.at[pl.ds(pl.multiple_of(start, mb), mb), :], zsem)

        for e in range(N_EXPERTS):
            @pl.when(pad_ref[e] > 0)
            def _():
                zero_block(pend_ref[e] - mb).start()

        def tail_start(j, c):
            zero_block(j * mb).start()
            return c
        lax.fori_loop(nu_ref[0], nblk, tail_start, 0)
        for e in range(N_EXPERTS):
            @pl.when(pad_ref[e] > 0)
            def _():
                zero_block(0).wait()

        def tail_wait(j, c):
            zero_block(0).wait()
            return c
        lax.fori_loop(nu_ref[0], nblk, tail_wait, 0)

    base = i * (tm * TOP_K)
    for r in range(tm):
        for k in range(TOP_K):
            d = dest_ref[base + r * TOP_K + k]
            pltpu.make_async_copy(hp_ref.at[pl.ds(r, 1), :], xs_hbm.at[pl.ds(d, 1), :], sem).start(priority=k % 2)
    for k in range(TOP_K):
        pltpu.make_async_copy(hp_ref, xs_hbm.at[pl.ds(0, tm), :], sem).wait()


def _dispatch(hp, dest, pend, padded, n_used, n_slots):
    t = hp.shape[0]
    tm = min(DISPATCH_TM, t)
    grid_spec = pltpu.PrefetchScalarGridSpec(
        num_scalar_prefetch=4,
        grid=(t // tm,),
        in_specs=[pl.BlockSpec((tm, HALF_D), lambda i, ds, pe, pa, nu: (i, 0))],
        out_specs=pl.BlockSpec(memory_space=pl.ANY),
        scratch_shapes=[pltpu.VMEM((MOE_BLOCK, HALF_D), jnp.uint32), pltpu.SemaphoreType.DMA(()),
                        pltpu.SemaphoreType.DMA(())],
    )
    return pl.pallas_call(
        _dispatch_kernel,
        grid_spec=grid_spec,
        out_shape=jax.ShapeDtypeStruct((n_slots, HALF_D), jnp.uint32),
        compiler_params=_cparams(("arbitrary",)),
        name="moe_dispatch",
    )(dest.reshape(-1), pend, padded, n_used, hp)


def _moe_kernel(be_ref, nu_ref, xs_ref, wgu_ref, bgu_ref, wd_ref, bd_ref, o_ref):
    i = pl.program_id(0)
    nused = nu_ref[0]

    @pl.when(i < nused)
    def _():
        lo, hi = _unpack_pairs(xs_ref[...])
        x = jnp.concatenate([lo.astype(BF16), hi.astype(BF16)], axis=1)
        gu = _dot(x, wgu_ref[...]) + bgu_ref[...]
        gate = jnp.minimum(gu[:, :EXPERT_DIM], SWIGLU_LIMIT)
        up = jnp.clip(gu[:, EXPERT_DIM:], -SWIGLU_LIMIT, SWIGLU_LIMIT)
        act = (up + 1.0) * gate * _sigmoid(gate * SWIGLU_ALPHA)
        ye = _dot(act.astype(BF16), wd_ref[...]) + bd_ref[...]
        o_ref[...] = _pack_pairs(ye[:, :HALF_D], ye[:, HALF_D:])

    @pl.when(i >= nused)
    def _():
        o_ref[...] = jnp.zeros_like(o_ref)


def _moe_experts(xs, block_e, n_used, w_gu, b_gu, w_down, b_down, layer):
    d = D_MODEL
    nblk = block_e.shape[0]
    mb = MOE_BLOCK
    wsel = lambda i, be, nu: (layer, be[i], 0, 0)
    grid_spec = pltpu.PrefetchScalarGridSpec(
        num_scalar_prefetch=2,
        grid=(nblk,),
        in_specs=[
            pl.BlockSpec((mb, HALF_D), lambda i, be, nu: (jnp.minimum(i, jnp.maximum(nu[0] - 1, 0)), 0)),
            pl.BlockSpec((None, None, d, 2 * EXPERT_DIM), wsel),
            pl.BlockSpec((None, None, 1, 2 * EXPERT_DIM), wsel),
            pl.BlockSpec((None, None, EXPERT_DIM, d), wsel),
            pl.BlockSpec((None, None, 1, d), wsel),
        ],
        out_specs=pl.BlockSpec((mb, HALF_D), lambda i, be, nu: (i, 0)),
    )
    depth = w_gu.shape[0]
    return pl.pallas_call(
        _moe_kernel,
        grid_spec=grid_spec,
        out_shape=jax.ShapeDtypeStruct((nblk * mb, HALF_D), jnp.uint32),
        compiler_params=_cparams(("arbitrary",)),
        name="moe_experts",
    )(block_e, n_used, xs, w_gu, b_gu.reshape(depth, N_EXPERTS, 1, -1), w_down,
      b_down.reshape(depth, N_EXPERTS, 1, -1))


def _combine_kernel(dest_ref, yb_hbm, wt_ref, h_ref, g_ref, b_ref, o_ref, ob_ref, gbuf, sem, *, alpha, tm):
    i = pl.program_id(0)
    n = pl.num_programs(0)
    rows = TOP_K * tm
    prio = (0, 1)

    @pl.when(i == 0)
    def _():
        _gather_rows(dest_ref, 0, 0, rows, yb_hbm, gbuf.at[0], sem.at[0], prio)

    @pl.when(i + 1 < n)
    def _():
        nxt = (i + 1) % 2
        _gather_rows(dest_ref, (i + 1) * rows, 0, rows, yb_hbm, gbuf.at[nxt], sem.at[nxt], prio)

    slot = i % 2
    _wait_rows(rows, yb_hbm, gbuf.at[slot], sem.at[slot])
    wt = wt_ref[...]
    flo = fhi = None
    for k in range(TOP_K):
        lo, hi = _unpack_pairs(gbuf[slot, k * tm:(k + 1) * tm, :])
        wk = wt[:, k:k + 1]
        flo = wk * lo if flo is None else flo + wk * lo
        fhi = wk * hi if fhi is None else fhi + wk * hi
    ff = jnp.concatenate([flo, fhi], axis=1)
    y = _ln_rows(alpha * h_ref[...] + ff, g_ref[...], b_ref[...])
    o_ref[...] = y
    ob_ref[...] = y.astype(BF16)


def _moe_combine(yb, dest_blk, top_w, h, g, b, alpha, tm):
    t, d = h.shape
    row = lambda i, ds: (i, 0)
    fix = lambda i, ds: (0, 0)
    grid_spec = pltpu.PrefetchScalarGridSpec(
        num_scalar_prefetch=1,
        grid=(t // tm,),
        in_specs=[pl.BlockSpec(memory_space=pl.ANY), pl.BlockSpec((tm, ROUTE_LANES), row),
                  pl.BlockSpec((tm, d), row), pl.BlockSpec((1, d), fix), pl.BlockSpec((1, d), fix)],
        out_specs=[pl.BlockSpec((tm, d), row), pl.BlockSpec((tm, d), row)],
        scratch_shapes=[pltpu.VMEM((2, TOP_K * tm, HALF_D), jnp.uint32), pltpu.SemaphoreType.DMA((2,))],
    )
    return pl.pallas_call(
        functools.partial(_combine_kernel, alpha=alpha, tm=tm),
        grid_spec=grid_spec,
        out_shape=[jax.ShapeDtypeStruct((t, d), F32), jax.ShapeDtypeStruct((t, d), BF16)],
        compiler_params=_cparams(("arbitrary",)),
        name="moe_combine",
    )(dest_blk, yb, top_w, h, g.reshape(1, d), b.reshape(1, d))


def _moe_layer(h, hp, w_router, b_router, w_gu, b_gu, w_down, b_down, layer, g, b, alpha):
    t, d = h.shape
    top_idx, rank, top_w, counts = _router(h, w_router, b_router)
    n_assign = t * TOP_K
    padded = ((counts + MOE_BLOCK - 1) // MOE_BLOCK * MOE_BLOCK).astype(jnp.int32)
    pend = jnp.cumsum(padded).astype(jnp.int32)
    pstart = pend - padded
    dest = (pstart[top_idx] + rank).astype(jnp.int32)
    n_blocks = -(-n_assign // MOE_BLOCK) + N_EXPERTS
    n_slots = n_blocks * MOE_BLOCK
    blk_start = jnp.arange(n_blocks, dtype=jnp.int32) * MOE_BLOCK
    block_e = jnp.minimum(jnp.sum((pend[None, :] <= blk_start[:, None]).astype(jnp.int32), axis=1),
                          N_EXPERTS - 1).astype(jnp.int32)
    n_used = (pend[-1] // MOE_BLOCK).astype(jnp.int32).reshape(1)
    xs = _dispatch(hp, dest, pend, padded, n_used, n_slots)
    yb = _moe_experts(xs, block_e, n_used, w_gu, b_gu, w_down, b_down, layer)
    tm = min(128, t)
    dest_blk = dest.reshape(t // tm, tm, TOP_K).transpose(0, 2, 1).reshape(-1)
    return _moe_combine(yb, dest_blk, top_w, h, g, b, alpha, tm)


def kernel(x, mem, positions, ln_in_g, ln_in_b, ln_mem_g, ln_mem_b, w_in, conv_w, conv_b, dt_bias, a_log, d_skip, ssd_norm_g, w_ret_o, w_ssd_o, w_mix_o, ln1_g, ln1_b, w_xq, w_xkv, w_xo, ln2_g, ln2_b, w_router, b_router, w_gu, b_gu, w_down, b_down, ln3_g, ln3_b):
    bsz, s, d = x.shape
    t = bsz * s
    depth = w_in.shape[0]
    alpha = (2.0 * depth) ** 0.25
    h, hb = _layernorm(x.reshape(t, d), ln_in_g, ln_in_b)
    _, memb = _layernorm(mem.reshape(bsz * N_MEM, d), ln_mem_g, ln_mem_b)
    cos, sin = _rope_tables(positions)
    w_in_b, w_ret_b, w_ssd_b, w_mix_b = (w.astype(BF16) for w in (w_in, w_ret_o, w_ssd_o, w_mix_o))
    w_xq_b, w_xkv_b, w_xo_b = (w.astype(BF16) for w in (w_xq, w_xkv, w_xo))
    w_gu_b, w_down_b = w_gu.astype(BF16), w_down.astype(BF16)
    for l in range(depth):
        proj, dtraw = _in_proj(hb, w_in_b, l)
        proj3 = proj.reshape(bsz, s, MAIN_W)
        ret = _retention(proj3, cos, sin)
        xbc = _conv_silu(proj3, conv_w[l], conv_b[l])
        ssd = _ssd(xbc, proj3, dtraw.reshape(bsz, s, 2 * SSD_HEADS), dt_bias[l], a_log[l], d_skip[l],
                   ssd_norm_g[l])
        merged = _merge(ret.reshape(t, RET_V), ssd.reshape(t, SSD_D_INNER), proj, w_ret_b, w_ssd_b, l)
        h, hb = _proj_res_ln(merged, w_mix_b, l, h, ln1_g[l], ln1_b[l], alpha)

        q = _matmul(hb, w_xq_b, l, BF16, 1024, 1024, "xattn_q")
        kv = _matmul(memb, w_xkv_b, l, BF16, 512, 1024, "xattn_kv")
        h3, hp3 = _xattn(q.reshape(bsz, s, d), kv.reshape(bsz, N_MEM, 2 * d), w_xo_b, l,
                         h.reshape(bsz, s, d), ln2_g[l], ln2_b[l], alpha)
        h = h3.reshape(t, d)

        h, hb = _moe_layer(h, hp3.reshape(t, HALF_D), w_router[l], b_router[l], w_gu_b, b_gu, w_down_b, b_down, l,
                           ln3_g[l], ln3_b[l], alpha)
    return h.reshape(bsz, s, d)
```

```python
import functools

import jax
import jax.numpy as jnp
from jax import lax
from jax.experimental import pallas as pl
from jax.experimental.pallas import tpu as pltpu

F32 = jnp.float32
BF16 = jnp.bfloat16

D_MODEL = 2048
N_MEM = 256
RET_HEADS = 8
RET_QK_DIM = 128
RET_V_DIM = 256
RET_Q = RET_HEADS * RET_QK_DIM
RET_V = RET_HEADS * RET_V_DIM
RET_CHUNK = 128
ROPE_BASE = 10000.0
SSD_D_INNER = 2 * D_MODEL
SSD_HEAD_DIM = 64
SSD_HEADS = SSD_D_INNER // SSD_HEAD_DIM
SSD_GROUPS = 8
SSD_HPG = SSD_HEADS // SSD_GROUPS
SSD_GROUP_W = SSD_D_INNER // SSD_GROUPS
SSD_STATE = 128
SSD_BC = SSD_GROUPS * SSD_STATE
SSD_CONV = 5
SSD_SUB = 64
SSD_GROUPS_PER_STEP = 2
CONV_CH = SSD_D_INNER + 2 * SSD_BC
XATTN_HEADS = 4
XATTN_HEAD_DIM = D_MODEL // XATTN_HEADS
N_EXPERTS = 32
TOP_K = 4
EXPERT_DIM = D_MODEL // 2
SWIGLU_LIMIT = 7.0
SWIGLU_ALPHA = 1.702
MOE_BLOCK = 256
LN_EPS = 1e-5
NEG_BIG = -1e30

OFF_Q = 0
OFF_K = OFF_Q + RET_Q
OFF_V = OFF_K + RET_Q
OFF_GRET = OFF_V + RET_V
OFF_Z = OFF_GRET + RET_V
OFF_XS = OFF_Z + SSD_D_INNER
OFF_BM = OFF_XS + SSD_D_INNER
OFF_CM = OFF_BM + SSD_BC
OFF_DT = OFF_CM + SSD_BC
OFF_GATE_R = OFF_DT + 2 * SSD_HEADS
OFF_GATE_S = OFF_GATE_R + D_MODEL
IN_WIDTH = OFF_GATE_S + D_MODEL
MAIN_W = IN_WIDTH - 2 * SSD_HEADS
M_GATE_R = OFF_DT
M_GATE_S = OFF_DT + D_MODEL

VMEM_LIMIT_MB = 48


def _cparams(sem, vmem_mb=VMEM_LIMIT_MB):
    return pltpu.CompilerParams(dimension_semantics=sem, vmem_limit_bytes=vmem_mb * 1024 * 1024)


def _dot(a, b):
    return jnp.dot(a, b, preferred_element_type=F32)


def _dot_nt(a, b):
    return lax.dot_general(a, b, (((1,), (1,)), ((), ())), preferred_element_type=F32)


def _dot_tn(a, b):
    return lax.dot_general(a, b, (((0,), (0,)), ((), ())), preferred_element_type=F32)


def _sigmoid(x):
    return 1.0 / (1.0 + jnp.exp(-x))


def _softplus(x):
    return jnp.maximum(x, 0.0) + jnp.log(1.0 + jnp.exp(-jnp.abs(x)))


def _split2(a):
    a1 = a.astype(BF16)
    return a1, (a - a1.astype(F32)).astype(BF16)


HALF_D = D_MODEL // 2
HI_MASK = 0xFFFF0000


def _pack_pairs(lo, hi):
    lo_bits = lax.bitcast_convert_type(lo.astype(BF16).astype(F32), jnp.uint32)
    hi_bits = lax.bitcast_convert_type(hi.astype(BF16).astype(F32), jnp.uint32)
    return (lo_bits >> 16) | (hi_bits & jnp.uint32(HI_MASK))


def _unpack_pairs(u):
    lo = lax.bitcast_convert_type(u << 16, F32)
    hi = lax.bitcast_convert_type(u & jnp.uint32(HI_MASK), F32)
    return lo, hi


def _ln_rows(x, g, b):
    mu = jnp.mean(x, axis=-1, keepdims=True)
    xc = x - mu
    var = jnp.mean(xc * xc, axis=-1, keepdims=True)
    return xc * lax.rsqrt(var + LN_EPS) * g + b


def _ln_kernel(x_ref, g_ref, b_ref, o_ref, ob_ref):
    y = _ln_rows(x_ref[...], g_ref[...], b_ref[...])
    o_ref[...] = y
    ob_ref[...] = y.astype(BF16)


def _layernorm(x, g, b):
    m, d = x.shape
    tm = min(512, m)
    return pl.pallas_call(
        _ln_kernel,
        grid=(m // tm,),
        in_specs=[pl.BlockSpec((tm, d), lambda i: (i, 0)),
                  pl.BlockSpec((1, d), lambda i: (0, 0)),
                  pl.BlockSpec((1, d), lambda i: (0, 0))],
        out_specs=[pl.BlockSpec((tm, d), lambda i: (i, 0)),
                   pl.BlockSpec((tm, d), lambda i: (i, 0))],
        out_shape=[jax.ShapeDtypeStruct((m, d), F32), jax.ShapeDtypeStruct((m, d), BF16)],
        compiler_params=_cparams(("parallel",)),
        name="layernorm",
    )(x, g.reshape(1, d), b.reshape(1, d))


def _mm_kernel(x_ref, w_ref, o_ref):
    o_ref[...] = _dot(x_ref[...], w_ref[...]).astype(o_ref.dtype)


def _matmul(x, w, layer, out_dtype, tm, tn, name):
    m, k = x.shape
    n = w.shape[2]
    tm = min(tm, m)
    tn = min(tn, n)
    return pl.pallas_call(
        _mm_kernel,
        grid=(m // tm, n // tn),
        in_specs=[pl.BlockSpec((tm, k), lambda i, j: (i, 0)),
                  pl.BlockSpec((None, k, tn), lambda i, j: (layer, 0, j))],
        out_specs=pl.BlockSpec((tm, tn), lambda i, j: (i, j)),
        out_shape=jax.ShapeDtypeStruct((m, n), out_dtype),
        compiler_params=_cparams(("parallel", "arbitrary")),
        name=name,
    )(x, w)


def _mm_lead_kernel(x_ref, w_ref, o_ref):
    o_ref[...] = _dot(x_ref[...], w_ref[0]).astype(o_ref.dtype)


def _in_proj(hb, w, layer):
    m, k = hb.shape
    tm = min(1024, m)
    tn = 1024
    nmain = OFF_DT // tn
    dtw = 2 * SSD_HEADS
    proj = pl.pallas_call(
        _mm_lead_kernel,
        grid=(m // tm, MAIN_W // tn),
        in_specs=[pl.BlockSpec((tm, k), lambda i, j: (i, 0)),
                  pl.BlockSpec((pl.Element(1), pl.Element(k), pl.Element(tn)),
                               lambda i, j: (layer, 0,
                                             pl.multiple_of(jnp.where(j < nmain, j * tn, j * tn + dtw), dtw)))],
        out_specs=pl.BlockSpec((tm, tn), lambda i, j: (i, j)),
        out_shape=jax.ShapeDtypeStruct((m, MAIN_W), BF16),
        compiler_params=_cparams(("parallel", "arbitrary")),
        name="in_proj",
    )(hb, w)
    dtraw = pl.pallas_call(
        _mm_kernel,
        grid=(m // tm,),
        in_specs=[pl.BlockSpec((tm, k), lambda i: (i, 0)),
                  pl.BlockSpec((None, k, dtw), lambda i: (layer, 0, OFF_DT // dtw))],
        out_specs=pl.BlockSpec((tm, dtw), lambda i: (i, 0)),
        out_shape=jax.ShapeDtypeStruct((m, dtw), F32),
        compiler_params=_cparams(("parallel",)),
        name="dt_proj",
    )(hb, w)
    return proj, dtraw


def _rope_kernel(pos_ref, inv_ref, sgn_ref, cos_ref, sin_ref):
    ang = pos_ref[...] * inv_ref[...]
    cos_ref[...] = jnp.cos(ang)
    sin_ref[...] = jnp.sin(ang) * sgn_ref[...]


def _rope_tables(positions):
    t = positions.size
    half = RET_QK_DIM // 2
    inv = ROPE_BASE ** (-jnp.arange(half, dtype=F32) / half)
    inv2 = jnp.concatenate([inv, inv]).reshape(1, RET_QK_DIM)
    sgn = jnp.concatenate([-jnp.ones((half,), F32), jnp.ones((half,), F32)]).reshape(1, RET_QK_DIM)
    pos = positions.astype(F32).reshape(t, 1)
    tm = min(1024, t)
    return pl.pallas_call(
        _rope_kernel,
        grid=(t // tm,),
        in_specs=[pl.BlockSpec((tm, 1), lambda i: (i, 0)),
                  pl.BlockSpec((1, RET_QK_DIM), lambda i: (0, 0)),
                  pl.BlockSpec((1, RET_QK_DIM), lambda i: (0, 0))],
        out_specs=[pl.BlockSpec((tm, RET_QK_DIM), lambda i: (i, 0)),
                   pl.BlockSpec((tm, RET_QK_DIM), lambda i: (i, 0))],
        out_shape=[jax.ShapeDtypeStruct((t, RET_QK_DIM), F32)] * 2,
        compiler_params=_cparams(("parallel",)),
        name="rope_tables",
    )(pos, inv2, sgn)


def _ret_kernel(*refs, reverse, nchunk):
    if not reverse:
        (q_ref, k_ref, v_ref, cos_ref, sin_ref, qd_ref, kd_ref, cd_ref, dm_ref, y_ref, st_ref) = refs
    else:
        (q_ref, k_ref, v_ref, cos_ref, sin_ref, qd_ref, kd_ref, cd_ref, yf_ref, g_ref, o_ref, st_ref) = refs

    @pl.when(pl.program_id(2) == 0)
    def _():
        st_ref[...] = jnp.zeros_like(st_ref)

    qd = qd_ref[...]
    kd = kd_ref[...]
    cd = cd_ref[...]
    scale = RET_QK_DIM ** -0.5
    c = RET_CHUNK
    order = range(nchunk - 1, -1, -1) if reverse else range(nchunk)
    st = st_ref[...]
    for ci in order:
        sl = slice(ci * c, (ci + 1) * c)
        cs = cos_ref[sl, :]
        sn = sin_ref[sl, :]
        q = q_ref[sl, :].astype(F32)
        k = k_ref[sl, :].astype(F32)
        v = v_ref[sl, :]
        qr = q * cs + pltpu.roll(q, RET_QK_DIM // 2, 1) * sn
        kr = (k * cs + pltpu.roll(k, RET_QK_DIM // 2, 1) * sn) * scale
        y = _dot((qr * qd).astype(BF16), st.astype(BF16))
        st = cd * st + _dot_tn((kr * kd).astype(BF16), v)
        if not reverse:
            s = _dot_nt(qr.astype(BF16), kr.astype(BF16)) * dm_ref[...]
            y_ref[sl, :] = y + _dot(s.astype(BF16), v)
        else:
            tot = yf_ref[sl, :] + y
            mu = jnp.mean(tot, axis=-1, keepdims=True)
            tc = tot - mu
            var = jnp.mean(tc * tc, axis=-1, keepdims=True)
            g = g_ref[sl, :].astype(F32)
            o_ref[sl, :] = (tc * lax.rsqrt(var + LN_EPS) * (g * _sigmoid(g))).astype(BF16)
    st_ref[...] = st


def _ret_decay_tables():
    lg = jnp.log1p(-jnp.exp2(-5.0 - jnp.arange(RET_HEADS, dtype=F32)))
    idx = jnp.arange(RET_CHUNK, dtype=F32)
    c = float(RET_CHUNK)
    dist = jnp.abs(idx[:, None] - idx[None, :])
    dmat = jnp.exp(lg[:, None, None] * dist)

    def rows(e):
        return jnp.broadcast_to(jnp.exp(lg[:, None] * e)[..., None], (RET_HEADS, RET_CHUNK, RET_QK_DIM))

    qd_f = rows(idx + 1.0)
    kd_f = rows(c - 1.0 - idx)
    qd_b = rows(c - idx)
    kd_b = rows(idx)
    cd = jnp.broadcast_to(jnp.exp(lg * c)[:, None, None], (RET_HEADS, 1, RET_V_DIM))
    return dmat, (qd_f, kd_f), (qd_b, kd_b), cd


def _retention(proj3, cos, sin):
    b, s, _ = proj3.shape
    rb = min(2048, s)
    nb = s // rb
    nchunk = rb // RET_CHUNK
    dmat, dec_f, dec_b, cd = _ret_decay_tables()
    cos3 = cos.reshape(b, s, RET_QK_DIM)
    sin3 = sin.reshape(b, s, RET_QK_DIM)
    kq = OFF_K // RET_QK_DIM
    kv = OFF_V // RET_V_DIM
    kg = OFF_GRET // RET_V_DIM
    tab = lambda w: pl.BlockSpec((None, RET_CHUNK, w), lambda bi, h, i: (h, 0, 0))
    cd_spec = pl.BlockSpec((None, 1, RET_V_DIM), lambda bi, h, i: (h, 0, 0))

    def specs(rev):
        blk = (lambda i: nb - 1 - i) if rev else (lambda i: i)
        return [
            pl.BlockSpec((None, rb, RET_QK_DIM), lambda bi, h, i: (bi, blk(i), h)),
            pl.BlockSpec((None, rb, RET_QK_DIM), lambda bi, h, i: (bi, blk(i), kq + h)),
            pl.BlockSpec((None, rb, RET_V_DIM), lambda bi, h, i: (bi, blk(i), kv + h)),
            pl.BlockSpec((None, rb, RET_QK_DIM), lambda bi, h, i: (bi, blk(i), 0)),
            pl.BlockSpec((None, rb, RET_QK_DIM), lambda bi, h, i: (bi, blk(i), 0)),
            tab(RET_QK_DIM), tab(RET_QK_DIM), cd_spec,
        ], blk

    sem = ("parallel", "parallel", "arbitrary")
    scratch = [pltpu.VMEM((RET_QK_DIM, RET_V_DIM), F32)]
    in_f, _ = specs(False)
    y_f = pl.pallas_call(
        functools.partial(_ret_kernel, reverse=False, nchunk=nchunk),
        grid=(b, RET_HEADS, nb),
        in_specs=in_f + [tab(RET_CHUNK)],
        out_specs=pl.BlockSpec((None, rb, RET_V_DIM), lambda bi, h, i: (bi, i, h)),
        out_shape=jax.ShapeDtypeStruct((b, s, RET_V), F32),
        scratch_shapes=scratch,
        compiler_params=_cparams(sem),
        name="retention_fwd",
    )(proj3, proj3, proj3, cos3, sin3, dec_f[0], dec_f[1], cd, dmat)
    in_b, blk = specs(True)
    out = pl.pallas_call(
        functools.partial(_ret_kernel, reverse=True, nchunk=nchunk),
        grid=(b, RET_HEADS, nb),
        in_specs=in_b + [
            pl.BlockSpec((None, rb, RET_V_DIM), lambda bi, h, i: (bi, blk(i), h)),
            pl.BlockSpec((None, rb, RET_V_DIM), lambda bi, h, i: (bi, blk(i), kg + h)),
        ],
        out_specs=pl.BlockSpec((None, rb, RET_V_DIM), lambda bi, h, i: (bi, blk(i), h)),
        out_shape=jax.ShapeDtypeStruct((b, s, RET_V), BF16),
        scratch_shapes=scratch,
        compiler_params=_cparams(sem),
        name="retention_bwd",
    )(proj3, proj3, proj3, cos3, sin3, dec_b[0], dec_b[1], cd, y_f, proj3)
    return out


CONV_HALO = 16


def _conv_kernel(xm_ref, xp_ref, xn_ref, w_ref, b_ref, o_ref):
    i = pl.program_id(1)
    n = pl.num_programs(1)
    tm = xm_ref.shape[0]
    pad = SSD_CONV // 2
    ext = jnp.concatenate([jnp.where(i == 0, 0.0, xp_ref[...].astype(F32)),
                           xm_ref[...].astype(F32),
                           jnp.where(i == n - 1, 0.0, xn_ref[...].astype(F32))], axis=0)
    rows = tm + 2 * CONV_HALO
    acc = b_ref[...] + w_ref[pad:pad + 1, :] * ext[CONV_HALO:CONV_HALO + tm, :]
    for k in range(SSD_CONV):
        if k != pad:
            sh = pltpu.roll(ext, (pad - k) % rows, 0)
            acc = acc + w_ref[k:k + 1, :] * sh[CONV_HALO:CONV_HALO + tm, :]
    o_ref[...] = (acc * _sigmoid(acc)).astype(BF16)


def _conv_silu(proj3, conv_w, conv_b):
    b, s, _ = proj3.shape
    tm = min(512, s)
    tc = 512
    ns = s // tm
    c0 = OFF_XS // tc
    hb = tm // CONV_HALO
    last = s // CONV_HALO - 1
    return pl.pallas_call(
        _conv_kernel,
        grid=(b, ns, CONV_CH // tc),
        in_specs=[
            pl.BlockSpec((None, tm, tc), lambda bi, i, j: (bi, i, c0 + j)),
            pl.BlockSpec((None, CONV_HALO, tc), lambda bi, i, j: (bi, jnp.maximum(i * hb - 1, 0), c0 + j)),
            pl.BlockSpec((None, CONV_HALO, tc), lambda bi, i, j: (bi, jnp.minimum((i + 1) * hb, last), c0 + j)),
            pl.BlockSpec((SSD_CONV, tc), lambda bi, i, j: (0, j)),
            pl.BlockSpec((1, tc), lambda bi, i, j: (0, j)),
        ],
        out_specs=pl.BlockSpec((None, tm, tc), lambda bi, i, j: (bi, i, j)),
        out_shape=jax.ShapeDtypeStruct((b, s, CONV_CH), BF16),
        compiler_params=_cparams(("parallel", "parallel", "parallel")),
        name="conv_silu",
    )(proj3, proj3, proj3, conv_w, conv_b.reshape(1, CONV_CH))


class _ColumnView:
    def __init__(self, ref, k, w):
        self.ref, self.lo, self.hi = ref, k * w, (k + 1) * w

    def _rows(self, idx):
        return slice(None) if idx is Ellipsis else idx[0]

    def __getitem__(self, idx):
        return self.ref[self._rows(idx), self.lo:self.hi]

    def __setitem__(self, idx, value):
        self.ref[self._rows(idx), self.lo:self.hi] = value


def _ssd_kernel(*refs, reverse, nchunk, ngrp):
    n = ngrp
    gw = SSD_GROUP_W
    cols = lambda ref, w: [_ColumnView(ref, k, w) for k in range(n)]
    x_refs, b_refs, c_refs = cols(refs[0], gw), cols(refs[1], SSD_STATE), cols(refs[2], SSD_STATE)
    dt_ref, bias_ref, alog_ref = refs[3:6]
    if not reverse:
        y_refs, h_refs = cols(refs[6], gw), refs[7:7 + n]
    else:
        yf_refs, z_refs, dsk_refs, ng_refs, o_refs = (cols(r, gw) for r in refs[6:11])
        h_refs = refs[11:11 + n]

    @pl.when(pl.program_id(2) == 0)
    def _():
        for h_ref in h_refs:
            h_ref[...] = jnp.zeros_like(h_ref)

    sub = SSD_SUB
    ch = 2 * sub
    i32 = jnp.int32
    ci = lax.broadcasted_iota(i32, (ch, gw), 0)
    li = lax.broadcasted_iota(i32, (ch, gw), 1)
    rn = lax.broadcasted_iota(i32, (16, ch), 0)
    cn = lax.broadcasted_iota(i32, (16, ch), 1)
    selws, selns = [], []
    for k in range(n):
        g = pl.program_id(1) * n + k
        base = (SSD_HEADS if reverse else 0) + g * SSD_HPG
        selws.append((ci == base + (li >> 6)).astype(BF16))
        selns.append(jnp.logical_and(cn == base + rn, rn < SSD_HPG).astype(BF16))
    ii = lax.broadcasted_iota(i32, (ch, ch), 0)
    jj = lax.broadcasted_iota(i32, (ch, ch), 1)
    tri = ((jj >= ii) if reverse else (jj <= ii)).astype(BF16)
    jh = jj & (sub - 1)
    bmask = (ii >> 6) == (jj >> 6)
    if reverse:
        mask_full = (jh + sub) > ii
        mask_half = (jh > ii)[:sub]
    else:
        mask_full = ii >= jh
        mask_half = (ii >= jh)[:sub]
    bias = bias_ref[...]
    a_neg = -jnp.exp(alog_ref[...])

    hsts = [h_ref[...] for h_ref in h_refs]

    def _ssd_group_chunk(k, rows, pieces, ps):
        wide = _dot(pieces, selws[k])
        dtw = wide[:ch] + wide[ch:2 * ch]
        pw = wide[2 * ch:3 * ch] + wide[3 * ch:]
        pn = sum(_dot_nt(selns[k], s) for s in ps)
        x = x_refs[k][rows, :].astype(F32)
        xdt = x * dtw
        bm = b_refs[k][rows, :]
        cm = c_refs[k][rows, :]
        bm_a = jnp.concatenate([bm[:sub], bm[:sub]], axis=0)
        bm_b = jnp.concatenate([bm[sub:], bm[sub:]], axis=0)
        if reverse:
            cb_full = _dot_nt(cm, bm_b)
            cb_half = _dot_nt(cm[:sub], bm_a)
        else:
            cb_full = _dot_nt(cm, bm_a)
            cb_half = _dot_nt(cm[sub:], bm_b)
        parts = []
        for p in range(SSD_HPG // 2):
            colp = pw[:, p * ch:(p + 1) * ch]
            row_a = jnp.concatenate([pn[2 * p:2 * p + 1, :sub], pn[2 * p + 1:2 * p + 2, :sub]], axis=1)
            row_b = jnp.concatenate([pn[2 * p:2 * p + 1, sub:], pn[2 * p + 1:2 * p + 2, sub:]], axis=1)
            xp = xdt[:, p * ch:(p + 1) * ch].astype(BF16)
            xb_a = jnp.where(bmask, jnp.concatenate([xp[:sub], xp[:sub]], axis=0), 0)
            xb_b = jnp.where(bmask, jnp.concatenate([xp[sub:], xp[sub:]], axis=0), 0)
            if reverse:
                w_full = (cb_full * jnp.exp(jnp.where(mask_full, colp - row_b, NEG_BIG))).astype(BF16)
                w_half = (cb_half * jnp.exp(jnp.where(mask_half, colp[:sub] - row_a, NEG_BIG))).astype(BF16)
                y_full = _dot(w_full, xb_b)
                y_half = _dot(w_half, xb_a)
                parts.append(jnp.concatenate([y_full[:sub] + y_half, y_full[sub:]], axis=0))
            else:
                w_full = (cb_full * jnp.exp(jnp.where(mask_full, colp - row_a, NEG_BIG))).astype(BF16)
                w_half = (cb_half * jnp.exp(jnp.where(mask_half, colp[sub:] - row_b, NEG_BIG))).astype(BF16)
                y_full = _dot(w_full, xb_a)
                y_half = _dot(w_half, xb_b)
                parts.append(jnp.concatenate([y_full[:sub], y_full[sub:] + y_half], axis=0))
        y = jnp.concatenate(parts, axis=1) + _dot(cm, hsts[k].astype(BF16)) * jnp.exp(pw)
        plast = pw[0:1, :] if reverse else pw[ch - 1:ch, :]
        xdec = (jnp.exp(plast - pw) * xdt).astype(BF16)
        hsts[k] = jnp.exp(plast) * hsts[k] + _dot_tn(bm, xdec)
        if not reverse:
            y_refs[k][rows, :] = y
        else:
            tot = yf_refs[k][rows, :] + y + dsk_refs[k][...] * x
            z = z_refs[k][rows, :].astype(F32)
            tot = tot * (z * _sigmoid(z))
            ms = jnp.mean(tot * tot, axis=-1, keepdims=True)
            o_refs[k][rows, :] = (tot * lax.rsqrt(ms + LN_EPS) * ng_refs[k][...]).astype(BF16)

    for t in range(nchunk):
        sc = (nchunk - 1 - t) if reverse else t
        rows = slice(sc * ch, (sc + 1) * ch)
        dt_all = _softplus(dt_ref[rows, :] + bias)
        p_all = sum(_dot(tri, s) for s in _split2(dt_all * a_neg))
        ps = _split2(p_all)
        pieces = jnp.concatenate(_split2(dt_all) + ps, axis=0)
        for k in range(n):
            _ssd_group_chunk(k, rows, pieces, ps)
    for h_ref, hst in zip(h_refs, hsts):
        h_ref[...] = hst


def _ssd(xbc, proj3, dtraw3, dt_bias, a_log, d_skip, norm_g):
    b, s, _ = xbc.shape
    rb = min(1024, s)
    nb = s // rb
    nchunk = rb // (2 * SSD_SUB)
    ngrp = SSD_GROUPS_PER_STEP
    gw = ngrp * SSD_GROUP_W
    sw = ngrp * SSD_STATE
    bias = dt_bias.astype(F32).reshape(1, 2 * SSD_HEADS)
    alog = a_log.astype(F32).reshape(1, 2 * SSD_HEADS)
    dsk = jnp.broadcast_to(d_skip.astype(F32)[:, None], (SSD_HEADS, SSD_HEAD_DIM)).reshape(SSD_GROUPS // ngrp, 1, gw)
    ng = norm_g.astype(F32).reshape(SSD_GROUPS // ngrp, 1, gw)
    kb = SSD_D_INNER // sw
    kc = (SSD_D_INNER + SSD_BC) // sw
    kz = OFF_Z // gw

    def specs(rev):
        blk = (lambda i: nb - 1 - i) if rev else (lambda i: i)
        small = pl.BlockSpec((1, 2 * SSD_HEADS), lambda bi, g, i: (0, 0))
        return [
            pl.BlockSpec((None, rb, gw), lambda bi, g, i: (bi, blk(i), g)),
            pl.BlockSpec((None, rb, sw), lambda bi, g, i: (bi, blk(i), kb + g)),
            pl.BlockSpec((None, rb, sw), lambda bi, g, i: (bi, blk(i), kc + g)),
            pl.BlockSpec((None, rb, 2 * SSD_HEADS), lambda bi, g, i: (bi, blk(i), 0)),
            small, small,
        ], blk

    sem = ("parallel", "parallel", "arbitrary")
    scratch = [pltpu.VMEM((SSD_STATE, SSD_GROUP_W), F32) for _ in range(ngrp)]
    in_f, _ = specs(False)
    y_f = pl.pallas_call(
        functools.partial(_ssd_kernel, reverse=False, nchunk=nchunk, ngrp=ngrp),
        grid=(b, SSD_GROUPS // ngrp, nb),
        in_specs=in_f,
        out_specs=pl.BlockSpec((None, rb, gw), lambda bi, g, i: (bi, i, g)),
        out_shape=jax.ShapeDtypeStruct((b, s, SSD_D_INNER), F32),
        scratch_shapes=scratch,
        compiler_params=_cparams(sem),
        name="ssd_fwd",
    )(xbc, xbc, xbc, dtraw3, bias, alog)
    in_b, blk = specs(True)
    grp = pl.BlockSpec((None, 1, gw), lambda bi, g, i: (g, 0, 0))
    out = pl.pallas_call(
        functools.partial(_ssd_kernel, reverse=True, nchunk=nchunk, ngrp=ngrp),
        grid=(b, SSD_GROUPS // ngrp, nb),
        in_specs=in_b + [
            pl.BlockSpec((None, rb, gw), lambda bi, g, i: (bi, blk(i), g)),
            pl.BlockSpec((None, rb, gw), lambda bi, g, i: (bi, blk(i), kz + g)),
            grp, grp,
        ],
        out_specs=pl.BlockSpec((None, rb, gw), lambda bi, g, i: (bi, blk(i), g)),
        out_shape=jax.ShapeDtypeStruct((b, s, SSD_D_INNER), BF16),
        scratch_shapes=scratch,
        compiler_params=_cparams(sem),
        name="ssd_bwd",
    )(xbc, xbc, xbc, dtraw3, bias, alog, y_f, proj3, dsk, ng)
    return out


def _merge_kernel(ret_ref, ssd_ref, wr_ref, ws_ref, gr_ref, gs_ref, o_ref):
    yr = _dot(ret_ref[...], wr_ref[...])
    ys = _dot(ssd_ref[...], ws_ref[...])
    o = _sigmoid(gr_ref[...].astype(F32)) * yr + _sigmoid(gs_ref[...].astype(F32)) * ys
    o_ref[...] = o.astype(BF16)


def _merge(ret, ssd, proj, w_ret_o, w_ssd_o, layer):
    m = ret.shape[0]
    tm = min(1024, m)
    tn = 256
    gr0 = M_GATE_R // tn
    gs0 = M_GATE_S // tn
    return pl.pallas_call(
        _merge_kernel,
        grid=(m // tm, D_MODEL // tn),
        in_specs=[
            pl.BlockSpec((tm, RET_V), lambda i, j: (i, 0)),
            pl.BlockSpec((tm, SSD_D_INNER), lambda i, j: (i, 0)),
            pl.BlockSpec((None, RET_V, tn), lambda i, j: (layer, 0, j)),
            pl.BlockSpec((None, SSD_D_INNER, tn), lambda i, j: (layer, 0, j)),
            pl.BlockSpec((tm, tn), lambda i, j: (i, gr0 + j)),
            pl.BlockSpec((tm, tn), lambda i, j: (i, gs0 + j)),
        ],
        out_specs=pl.BlockSpec((tm, tn), lambda i, j: (i, j)),
        out_shape=jax.ShapeDtypeStruct((m, D_MODEL), BF16),
        compiler_params=_cparams(("parallel", "arbitrary")),
        name="branch_merge",
    )(ret, ssd, w_ret_o, w_ssd_o, proj, proj)


def _proj_ln_kernel(x_ref, w_ref, h_ref, g_ref, b_ref, o_ref, ob_ref, *, alpha):
    y = _dot(x_ref[...], w_ref[...]) + alpha * h_ref[...]
    y = _ln_rows(y, g_ref[...], b_ref[...])
    o_ref[...] = y
    ob_ref[...] = y.astype(BF16)


def _proj_res_ln(x, w, layer, h, g, b, alpha):
    m, k = x.shape
    d = w.shape[2]
    tm = min(256, m)
    row = lambda i: (i, 0)
    fix = lambda i: (0, 0)
    return pl.pallas_call(
        functools.partial(_proj_ln_kernel, alpha=alpha),
        grid=(m // tm,),
        in_specs=[pl.BlockSpec((tm, k), row), pl.BlockSpec((None, k, d), lambda i: (layer, 0, 0)),
                  pl.BlockSpec((tm, d), row),
                  pl.BlockSpec((1, d), fix), pl.BlockSpec((1, d), fix)],
        out_specs=[pl.BlockSpec((tm, d), row), pl.BlockSpec((tm, d), row)],
        out_shape=[jax.ShapeDtypeStruct((m, d), F32), jax.ShapeDtypeStruct((m, d), BF16)],
        compiler_params=_cparams(("parallel",)),
        name="proj_residual_ln",
    )(x, w, h, g.reshape(1, d), b.reshape(1, d))


def _xattn_kernel(x_ref, wq_ref, k_ref, v_ref, wo_ref, h_ref, g_ref, b_ref, o_ref, op_ref, *, alpha):
    scale = XATTN_HEAD_DIM ** -0.5
    q = _dot(x_ref[...], wq_ref[...]).astype(BF16)
    outs = []
    for hd in range(XATTN_HEADS):
        sl = slice(hd * XATTN_HEAD_DIM, (hd + 1) * XATTN_HEAD_DIM)
        s = _dot_nt(q[:, sl], k_ref[:, sl]) * scale
        e = jnp.exp(s - jnp.max(s, axis=-1, keepdims=True))
        p = e / jnp.sum(e, axis=-1, keepdims=True)
        outs.append(_dot(p.astype(BF16), v_ref[:, sl]).astype(BF16))
    o = jnp.concatenate(outs, axis=1)
    y = _dot(o, wo_ref[...]) + alpha * h_ref[...]
    y = _ln_rows(y, g_ref[...], b_ref[...])
    o_ref[...] = y
    op_ref[...] = _pack_pairs(y[:, :HALF_D], y[:, HALF_D:])


def _xattn(x3, w_xq, kv3, w_xo, layer, h3, g, b, alpha):
    bsz, s, d = x3.shape
    tm = min(256, s)
    row = lambda bi, i: (bi, i, 0)
    fix = lambda bi, i: (0, 0)
    wspec = lambda: pl.BlockSpec((None, d, d), lambda bi, i: (layer, 0, 0), pipeline_mode=pl.Buffered(1))
    return pl.pallas_call(
        functools.partial(_xattn_kernel, alpha=alpha),
        grid=(bsz, s // tm),
        in_specs=[
            pl.BlockSpec((None, tm, d), row),
            wspec(),
            pl.BlockSpec((None, N_MEM, d), lambda bi, i: (bi, 0, 0)),
            pl.BlockSpec((None, N_MEM, d), lambda bi, i: (bi, 0, 1)),
            wspec(),
            pl.BlockSpec((None, tm, d), row),
            pl.BlockSpec((1, d), fix), pl.BlockSpec((1, d), fix),
        ],
        out_specs=[pl.BlockSpec((None, tm, d), row), pl.BlockSpec((None, tm, HALF_D), row)],
        out_shape=[jax.ShapeDtypeStruct((bsz, s, d), F32), jax.ShapeDtypeStruct((bsz, s, HALF_D), jnp.uint32)],
        compiler_params=_cparams(("parallel", "parallel")),
        name="memory_xattn",
    )(x3, w_xq, kv3, kv3, w_xo, h3, g.reshape(1, d), b.reshape(1, d))


ROUTE_LANES = 128


def _router_kernel(h_ref, w_ref, b_ref, idx_ref, wt_ref, cnt_ref):
    h1, h2 = _split2(h_ref[...])
    w1, w2 = _split2(w_ref[...])
    lg = _dot(h1, w1) + (_dot(h1, w2) + _dot(h2, w1)) + b_ref[...]
    tm = lg.shape[0]
    lane = lax.broadcasted_iota(jnp.int32, (tm, N_EXPERTS), 1).astype(F32)
    vals, idxs, hits = [], [], []
    for _ in range(TOP_K):
        m = jnp.max(lg, axis=-1, keepdims=True)
        am = jnp.min(jnp.where(lg == m, lane, float(N_EXPERTS)), axis=-1, keepdims=True)
        vals.append(m)
        idxs.append(am.astype(jnp.int32))
        hits.append(lane == am)
        lg = jnp.where(hits[-1], -jnp.inf, lg)
    es = [jnp.exp(v - vals[0]) for v in vals]
    tot = es[0] + es[1] + es[2] + es[3]

    @pl.when(pl.program_id(0) == 0)
    def _():
        cnt_ref[...] = jnp.zeros_like(cnt_ref)

    tokhot = sum(h.astype(F32) for h in hits)
    ri = lax.broadcasted_iota(jnp.int32, (tm, tm), 0)
    ci = lax.broadcasted_iota(jnp.int32, (tm, tm), 1)
    before = _dot((ci < ri).astype(BF16), tokhot.astype(BF16)) + cnt_ref[...]
    ranks = [jnp.sum(jnp.where(h, before, 0.0), axis=-1, keepdims=True).astype(jnp.int32) for h in hits]
    cnt_ref[...] = cnt_ref[...] + jnp.sum(tokhot, axis=0, keepdims=True)

    out_lane = lax.broadcasted_iota(jnp.int32, (tm, ROUTE_LANES), 1)
    io = jnp.zeros((tm, ROUTE_LANES), jnp.int32)
    wo = jnp.zeros((tm, ROUTE_LANES), F32)
    for k in range(TOP_K):
        io = jnp.where(out_lane == k, idxs[k], io)
        io = jnp.where(out_lane == TOP_K + k, ranks[k], io)
        wo = jnp.where(out_lane == k, es[k] / tot, wo)
    idx_ref[...] = io
    wt_ref[...] = wo


def _router(h, w_router, b_router):
    m, d = h.shape
    tm = min(512, m)
    row = lambda i: (i, 0)
    fix = lambda i: (0, 0)
    idx, wt, cnt = pl.pallas_call(
        _router_kernel,
        grid=(m // tm,),
        in_specs=[pl.BlockSpec((tm, d), row), pl.BlockSpec((d, N_EXPERTS), fix),
                  pl.BlockSpec((1, N_EXPERTS), fix)],
        out_specs=[pl.BlockSpec((tm, ROUTE_LANES), row), pl.BlockSpec((tm, ROUTE_LANES), row),
                   pl.BlockSpec((1, N_EXPERTS), fix)],
        out_shape=[jax.ShapeDtypeStruct((m, ROUTE_LANES), jnp.int32),
                   jax.ShapeDtypeStruct((m, ROUTE_LANES), F32),
                   jax.ShapeDtypeStruct((1, N_EXPERTS), F32)],
        compiler_params=_cparams(("arbitrary",)),
        name="router_topk",
    )(h, w_router, b_router.reshape(1, N_EXPERTS))
    return idx[:, :TOP_K], idx[:, TOP_K:2 * TOP_K], wt, cnt.reshape(N_EXPERTS).astype(jnp.int32)


def _gather_rows(idx_ref, base, r0, n, src_hbm, dst, sem, priorities):
    for r in range(r0, r0 + n):
        row = idx_ref[base + r]
        pltpu.make_async_copy(src_hbm.at[pl.ds(row, 1), :], dst.at[pl.ds(r, 1), :], sem).start(
            priority=priorities[r % len(priorities)])


def _wait_rows(n, src_hbm, dst, sem):
    pltpu.make_async_copy(src_hbm.at[pl.ds(0, n), :], dst, sem).wait()


DISPATCH_TM = 256


def _dispatch_kernel(dest_ref, pend_ref, pad_ref, nu_ref, hp_ref, xs_hbm, zbuf, zsem, sem):
    i = pl.program_id(0)
    tm = hp_ref.shape[0]
    mb = MOE_BLOCK
    nblk = xs_hbm.shape[0] // mb

    @pl.when(i == 0)
    def _():
        zbuf[...] = jnp.zeros_like(zbuf)

        def zero_block(start):
            return pltpu.make_async_copy(zbuf, xs_hbm.at[pl.ds(pl.multiple_of(start, mb), mb), :], zsem)

        for e in range(N_EXPERTS):
            @pl.when(pad_ref[e] > 0)
            def _():
                zero_block(pend_ref[e] - mb).start()

        def tail_start(j, c):
            zero_block(j * mb).start()
            return c
        lax.fori_loop(nu_ref[0], nblk, tail_start, 0)
        for e in range(N_EXPERTS):
            @pl.when(pad_ref[e] > 0)
            def _():
                zero_block(0).wait()

        def tail_wait(j, c):
            zero_block(0).wait()
            return c
        lax.fori_loop(nu_ref[0], nblk, tail_wait, 0)

    base = i * (tm * TOP_K)
    for r in range(tm):
        for k in range(TOP_K):
            d = dest_ref[base + r * TOP_K + k]
            pltpu.make_async_copy(hp_ref.at[pl.ds(r, 1), :], xs_hbm.at[pl.ds(d, 1), :], sem).start(priority=k % 2)
    for k in range(TOP_K):
        pltpu.make_async_copy(hp_ref, xs_hbm.at[pl.ds(0, tm), :], sem).wait()


def _dispatch(hp, dest, pend, padded, n_used, n_slots):
    t = hp.shape[0]
    tm = min(DISPATCH_TM, t)
    grid_spec = pltpu.PrefetchScalarGridSpec(
        num_scalar_prefetch=4,
        grid=(t // tm,),
        in_specs=[pl.BlockSpec((tm, HALF_D), lambda i, ds, pe, pa, nu: (i, 0))],
        out_specs=pl.BlockSpec(memory_space=pl.ANY),
        scratch_shapes=[pltpu.VMEM((MOE_BLOCK, HALF_D), jnp.uint32), pltpu.SemaphoreType.DMA(()),
                        pltpu.SemaphoreType.DMA(())],
    )
    return pl.pallas_call(
        _dispatch_kernel,
        grid_spec=grid_spec,
        out_shape=jax.ShapeDtypeStruct((n_slots, HALF_D), jnp.uint32),
        compiler_params=_cparams(("arbitrary",)),
        name="moe_dispatch",
    )(dest.reshape(-1), pend, padded, n_used, hp)


def _moe_kernel(be_ref, nu_ref, xs_ref, wgu_ref, bgu_ref, wd_ref, bd_ref, o_ref):
    i = pl.program_id(0)
    nused = nu_ref[0]

    @pl.when(i < nused)
    def _():
        lo, hi = _unpack_pairs(xs_ref[...])
        x = jnp.concatenate([lo.astype(BF16), hi.astype(BF16)], axis=1)
        gu = _dot(x, wgu_ref[...]) + bgu_ref[...]
        gate = jnp.minimum(gu[:, :EXPERT_DIM], SWIGLU_LIMIT)
        up = jnp.clip(gu[:, EXPERT_DIM:], -SWIGLU_LIMIT, SWIGLU_LIMIT)
        act = (up + 1.0) * gate * _sigmoid(gate * SWIGLU_ALPHA)
        ye = _dot(act.astype(BF16), wd_ref[...]) + bd_ref[...]
        o_ref[...] = _pack_pairs(ye[:, :HALF_D], ye[:, HALF_D:])

    @pl.when(i >= nused)
    def _():
        o_ref[...] = jnp.zeros_like(o_ref)


def _moe_experts(xs, block_e, n_used, w_gu, b_gu, w_down, b_down, layer):
    d = D_MODEL
    nblk = block_e.shape[0]
    mb = MOE_BLOCK
    wsel = lambda i, be, nu: (layer, be[i], 0, 0)
    grid_spec = pltpu.PrefetchScalarGridSpec(
        num_scalar_prefetch=2,
        grid=(nblk,),
        in_specs=[
            pl.BlockSpec((mb, HALF_D), lambda i, be, nu: (jnp.minimum(i, jnp.maximum(nu[0] - 1, 0)), 0)),
            pl.BlockSpec((None, None, d, 2 * EXPERT_DIM), wsel),
            pl.BlockSpec((None, None, 1, 2 * EXPERT_DIM), wsel),
            pl.BlockSpec((None, None, EXPERT_DIM, d), wsel),
            pl.BlockSpec((None, None, 1, d), wsel),
        ],
        out_specs=pl.BlockSpec((mb, HALF_D), lambda i, be, nu: (i, 0)),
    )
    depth = w_gu.shape[0]
    return pl.pallas_call(
        _moe_kernel,
        grid_spec=grid_spec,
        out_shape=jax.ShapeDtypeStruct((nblk * mb, HALF_D), jnp.uint32),
        compiler_params=_cparams(("arbitrary",)),
        name="moe_experts",
    )(block_e, n_used, xs, w_gu, b_gu.reshape(depth, N_EXPERTS, 1, -1), w_down,
      b_down.reshape(depth, N_EXPERTS, 1, -1))


def _combine_kernel(dest_ref, yb_hbm, wt_ref, h_ref, g_ref, b_ref, o_ref, ob_ref, gbuf, sem, *, alpha, tm):
    i = pl.program_id(0)
    n = pl.num_programs(0)
    rows = TOP_K * tm
    prio = (0, 1)

    @pl.when(i == 0)
    def _():
        _gather_rows(dest_ref, 0, 0, rows, yb_hbm, gbuf.at[0], sem.at[0], prio)

    @pl.when(i + 1 < n)
    def _():
        nxt = (i + 1) % 2
        _gather_rows(dest_ref, (i + 1) * rows, 0, rows, yb_hbm, gbuf.at[nxt], sem.at[nxt], prio)

    slot = i % 2
    _wait_rows(rows, yb_hbm, gbuf.at[slot], sem.at[slot])
    wt = wt_ref[...]
    flo = fhi = None
    for k in range(TOP_K):
        lo, hi = _unpack_pairs(gbuf[slot, k * tm:(k + 1) * tm, :])
        wk = wt[:, k:k + 1]
        flo = wk * lo if flo is None else flo + wk * lo
        fhi = wk * hi if fhi is None else fhi + wk * hi
    ff = jnp.concatenate([flo, fhi], axis=1)
    y = _ln_rows(alpha * h_ref[...] + ff, g_ref[...], b_ref[...])
    o_ref[...] = y
    ob_ref[...] = y.astype(BF16)


def _moe_combine(yb, dest_blk, top_w, h, g, b, alpha, tm):
    t, d = h.shape
    row = lambda i, ds: (i, 0)
    fix = lambda i, ds: (0, 0)
    grid_spec = pltpu.PrefetchScalarGridSpec(
        num_scalar_prefetch=1,
        grid=(t // tm,),
        in_specs=[pl.BlockSpec(memory_space=pl.ANY), pl.BlockSpec((tm, ROUTE_LANES), row),
                  pl.BlockSpec((tm, d), row), pl.BlockSpec((1, d), fix), pl.BlockSpec((1, d), fix)],
        out_specs=[pl.BlockSpec((tm, d), row), pl.BlockSpec((tm, d), row)],
        scratch_shapes=[pltpu.VMEM((2, TOP_K * tm, HALF_D), jnp.uint32), pltpu.SemaphoreType.DMA((2,))],
    )
    return pl.pallas_call(
        functools.partial(_combine_kernel, alpha=alpha, tm=tm),
        grid_spec=grid_spec,
        out_shape=[jax.ShapeDtypeStruct((t, d), F32), jax.ShapeDtypeStruct((t, d), BF16)],
        compiler_params=_cparams(("arbitrary",)),
        name="moe_combine",
    )(dest_blk, yb, top_w, h, g.reshape(1, d), b.reshape(1, d))


def _moe_layer(h, hp, w_router, b_router, w_gu, b_gu, w_down, b_down, layer, g, b, alpha):
    t, d = h.shape
    top_idx, rank, top_w, counts = _router(h, w_router, b_router)
    n_assign = t * TOP_K
    padded = ((counts + MOE_BLOCK - 1) // MOE_BLOCK * MOE_BLOCK).astype(jnp.int32)
    pend = jnp.cumsum(padded).astype(jnp.int32)
    pstart = pend - padded
    dest = (pstart[top_idx] + rank).astype(jnp.int32)
    n_blocks = -(-n_assign // MOE_BLOCK) + N_EXPERTS
    n_slots = n_blocks * MOE_BLOCK
    blk_start = jnp.arange(n_blocks, dtype=jnp.int32) * MOE_BLOCK
    block_e = jnp.minimum(jnp.sum((pend[None, :] <= blk_start[:, None]).astype(jnp.int32), axis=1),
                          N_EXPERTS - 1).astype(jnp.int32)
    n_used = (pend[-1] // MOE_BLOCK).astype(jnp.int32).reshape(1)
    xs = _dispatch(hp, dest, pend, padded, n_used, n_slots)
    yb = _moe_experts(xs, block_e, n_used, w_gu, b_gu, w_down, b_down, layer)
    tm = min(128, t)
    dest_blk = dest.reshape(t // tm, tm, TOP_K).transpose(0, 2, 1).reshape(-1)
    return _moe_combine(yb, dest_blk, top_w, h, g, b, alpha, tm)


def kernel(x, mem, positions, ln_in_g, ln_in_b, ln_mem_g, ln_mem_b, w_in, conv_w, conv_b, dt_bias, a_log, d_skip, ssd_norm_g, w_ret_o, w_ssd_o, w_mix_o, ln1_g, ln1_b, w_xq, w_xkv, w_xo, ln2_g, ln2_b, w_router, b_router, w_gu, b_gu, w_down, b_down, ln3_g, ln3_b):
    bsz, s, d = x.shape
    t = bsz * s
    depth = w_in.shape[0]
    alpha = (2.0 * depth) ** 0.25
    h, hb = _layernorm(x.reshape(t, d), ln_in_g, ln_in_b)
    _, memb = _layernorm(mem.reshape(bsz * N_MEM, d), ln_mem_g, ln_mem_b)
    cos, sin = _rope_tables(positions)
    w_in_b, w_ret_b, w_ssd_b, w_mix_b = (w.astype(BF16) for w in (w_in, w_ret_o, w_ssd_o, w_mix_o))
    w_xq_b, w_xkv_b, w_xo_b = (w.astype(BF16) for w in (w_xq, w_xkv, w_xo))
    w_gu_b, w_down_b = w_gu.astype(BF16), w_down.astype(BF16)
    for l in range(depth):
        proj, dtraw = _in_proj(hb, w_in_b, l)
        proj3 = proj.reshape(bsz, s, MAIN_W)
        ret = _retention(proj3, cos, sin)
        xbc = _conv_silu(proj3, conv_w[l], conv_b[l])
        ssd = _ssd(xbc, proj3, dtraw.reshape(bsz, s, 2 * SSD_HEADS), dt_bias[l], a_log[l], d_skip[l],
                   ssd_norm_g[l])
        merged = _merge(ret.reshape(t, RET_V), ssd.reshape(t, SSD_D_INNER), proj, w_ret_b, w_ssd_b, l)
        h, hb = _proj_res_ln(merged, w_mix_b, l, h, ln1_g[l], ln1_b[l], alpha)

        kv = _matmul(memb, w_xkv_b, l, BF16, 512, 1024, "xattn_kv")
        h3, hp3 = _xattn(hb.reshape(bsz, s, d), w_xq_b, kv.reshape(bsz, N_MEM, 2 * d), w_xo_b, l,
                         h.reshape(bsz, s, d), ln2_g[l], ln2_b[l], alpha)
        h = h3.reshape(t, d)

        h, hb = _moe_layer(h, hp3.reshape(t, HALF_D), w_router[l], b_router[l], w_gu_b, b_gu, w_down_b, b_down, l,
                           ln3_g[l], ln3_b[l], alpha)
    return h.reshape(bsz, s, d)
```
